```python
import jax, jax.numpy as jnp
from jax import lax
import numpy as np

D_MODEL = 1024
BATCH = 8
SEQ = 2048
DEPTH = 2
DEC_BATCH = 128
DEC_SEQ = 4
PAST_LEN = 16384
PAGE_SIZE = 128

N_MIXERS = 2
N_GLA = (DEPTH + 1) // 2
N_HGRN = DEPTH // 2
CHUNK = 64
GLA_HEADS = 4
GLA_KEY_DIM = D_MODEL // 2
GLA_VAL_DIM = D_MODEL
GLA_DK = GLA_KEY_DIM // GLA_HEADS
GLA_DV = GLA_VAL_DIM // GLA_HEADS
GLA_GATE_RANK = 16
GLA_GATE_NORM = 16.0
HGRN_EXPAND = 128
HGRN_HEADS = D_MODEL // HGRN_EXPAND
HGRN_F_DIM = D_MODEL
HGRN_I_DIM = D_MODEL
HGRN_DF = HGRN_F_DIM // HGRN_HEADS
HGRN_DI = HGRN_I_DIM // HGRN_HEADS
D_FF = 2816
PLE_DIM = 256
RMS_EPS = 1e-6

kernel_name = 'hybrid_gla_hgrn2_macaron_step'


def rms_norm(x, w):
    xf = x.astype(jnp.float32)
    y = xf * lax.rsqrt(jnp.mean(xf * xf, axis=-1, keepdims=True) + RMS_EPS)
    return (y * w.astype(jnp.float32)).astype(x.dtype)


def swiglu(h, w_gu, w_down):
    gate, up = jnp.split(h @ w_gu, 2, axis=-1)
    return (jax.nn.silu(gate) * up) @ w_down


def chunked_gated_linear_recurrence(q, k, v, g, s0, scale):
    B, T, H, K = q.shape
    V = v.shape[-1]
    c = min(CHUNK, T)
    n = -(-T // c)
    pad = n * c - T
    f32 = jnp.float32
    q, k, v, g = (a.astype(f32) for a in (q, k, v, g))
    if pad:
        pw = ((0, 0), (0, pad), (0, 0), (0, 0))
        q, k, v, g = (jnp.pad(a, pw) for a in (q, k, v, g))
    to_chunks = lambda a: jnp.moveaxis(a.reshape(B, n, c, H, a.shape[-1]), 1, 0)
    mask = jnp.tril(jnp.ones((c, c), dtype=bool))[None, :, :, None, None]

    def step(S, inp):
        qc, kc, vc, gc = inp
        b = jnp.cumsum(gc, axis=1)
        rel = b[:, :, None] - b[:, None, :]
        decay = jnp.exp(jnp.where(mask, rel, -jnp.inf))
        attn = jnp.einsum('bthk,bshk,btshk->bhts', qc, kc, decay) * scale
        o_intra = jnp.einsum('bhts,bshv->bthv', attn, vc)
        o_inter = jnp.einsum('bthk,bhkv->bthv', qc * jnp.exp(b), S) * scale
        b_last = b[:, -1]
        k_dec = kc * jnp.exp(b_last[:, None] - b)
        S_new = S * jnp.exp(b_last)[..., None] + jnp.einsum('bshk,bshv->bhkv', k_dec, vc)
        return S_new, o_intra + o_inter

    S, o = lax.scan(step, s0.astype(f32), (to_chunks(q), to_chunks(k), to_chunks(v), to_chunks(g)))
    o = jnp.moveaxis(o, 0, 1).reshape(B, n * c, H, V)[:, :T]
    return o, S.astype(s0.dtype)


def gla_mixer(u, w_in, w_gk_up, b_gk, gn, w_out, s0):
    B, T, _ = u.shape
    z = u @ w_in
    cuts = np.cumsum([GLA_KEY_DIM, GLA_KEY_DIM, GLA_VAL_DIM, GLA_VAL_DIM]).tolist()
    q, k, v, g, r = jnp.split(z, cuts, axis=-1)
    gk = jax.nn.log_sigmoid((r @ w_gk_up + b_gk).astype(jnp.float32)) / GLA_GATE_NORM
    heads = lambda a, d: a.reshape(B, T, GLA_HEADS, d)
    o, s = chunked_gated_linear_recurrence(heads(q, GLA_DK), heads(k, GLA_DK), heads(v, GLA_DV),
                                           heads(gk, GLA_DK), s0, GLA_DK ** -0.5)
    o = rms_norm(o.astype(u.dtype), gn) * jax.nn.silu(heads(g, GLA_DV))
    return o.reshape(B, T, GLA_VAL_DIM) @ w_out, s


def hgrn2_mixer(u, w_in, gn, w_out, lb, s0):
    B, T, _ = u.shape
    z = u @ w_in
    cuts = [HGRN_F_DIM, 2 * HGRN_F_DIM, 2 * HGRN_F_DIM + HGRN_I_DIM]
    q, f, i, g = jnp.split(z, cuts, axis=-1)
    q = jax.nn.silu(q)
    f = f.astype(jnp.float32)
    log_f = jnp.logaddexp(jnp.log(lb), jnp.log1p(-lb) + jax.nn.log_sigmoid(f))
    k = (1.0 - lb) * jax.nn.sigmoid(-f)
    heads = lambda a, d: a.reshape(B, T, HGRN_HEADS, d)
    o, s = chunked_gated_linear_recurrence(heads(q, HGRN_DF), heads(k, HGRN_DF), heads(i, HGRN_DI),
                                           heads(log_f, HGRN_DF), s0, 1.0)
    o = rms_norm(o.astype(u.dtype), gn) * jax.nn.silu(heads(g, HGRN_DI))
    return o.reshape(B, T, HGRN_I_DIM) @ w_out, s


def trunk(x, p, st_gla, st_hgrn, ffn1_norm, ffn1_w_gu, ffn1_w_down, mix_norm,
          gla_w_in, gla_w_gk_up, gla_b_gk, gla_gn, gla_w_out,
          hgrn_w_in, hgrn_gn, hgrn_w_out, hgrn_lower_bounds,
          ffn2_norm, ffn2_w_gu, ffn2_w_down, ple_norm, ple_w_gate, ple_w_proj, final_norm):
    lbs = jax.nn.softmax(hgrn_lower_bounds.astype(jnp.float32), axis=0)
    lbs = jnp.cumsum(lbs, axis=0) - lbs[0]
    new_gla, new_hgrn = [], []
    h = x
    for li in range(DEPTH):
        h = h + 0.5 * swiglu(rms_norm(h, ffn1_norm[li]), ffn1_w_gu[li], ffn1_w_down[li])
        u = rms_norm(h, mix_norm[li])
        j = li // N_MIXERS
        if li % N_MIXERS == 0:
            m, s = gla_mixer(u, gla_w_in[j], gla_w_gk_up[j], gla_b_gk[j], gla_gn[j], gla_w_out[j], st_gla[j])
            new_gla.append(s)
        else:
            m, s = hgrn2_mixer(u, hgrn_w_in[j], hgrn_gn[j], hgrn_w_out[j], lbs[li], st_hgrn[j])
            new_hgrn.append(s)
        h = h + m
        h = h + 0.5 * swiglu(rms_norm(h, ffn2_norm[li]), ffn2_w_gu[li], ffn2_w_down[li])
        gate = jax.nn.sigmoid(rms_norm(h, ple_norm[li]) @ ple_w_gate[li])
        h = h + gate * (p[li] @ ple_w_proj[li])
    return rms_norm(h, final_norm), jnp.stack(new_gla), jnp.stack(new_hgrn)


def setup_inputs(seed: int = 0) -> dict:
    key = jax.random.key(seed)
    ks = iter(jax.random.split(key, 40))
    nrm = lambda shape, scale: jax.random.normal(next(ks), shape, jnp.float32) * scale
    gain = lambda shape: 1.0 + nrm(shape, 0.02)
    D = D_MODEL
    gla_in_w = 2 * GLA_KEY_DIM + 2 * GLA_VAL_DIM + GLA_GATE_RANK
    hgrn_in_w = 2 * HGRN_F_DIM + 2 * HGRN_I_DIM
    return {
        'x_prompt': nrm((BATCH, SEQ, D), 1.0),
        'x_sample': nrm((DEC_BATCH, DEC_SEQ, D), 1.0),
        'state_gla': nrm((N_GLA, DEC_BATCH, GLA_HEADS, GLA_DK, GLA_DV), 0.5),
        'state_hgrn': nrm((N_HGRN, DEC_BATCH, HGRN_HEADS, HGRN_DF, HGRN_DI), 0.5),
        'p_prompt': nrm((DEPTH, BATCH, SEQ, PLE_DIM), 1.0),
        'p_sample': nrm((DEPTH, DEC_BATCH, DEC_SEQ, PLE_DIM), 1.0),
        'ffn1_norm': gain((DEPTH, D)),
        'ffn1_w_gu': nrm((DEPTH, D, 2 * D_FF), D ** -0.5),
        'ffn1_w_down': nrm((DEPTH, D_FF, D), D_FF ** -0.5),
        'mix_norm': gain((DEPTH, D)),
        'gla_w_in': nrm((N_GLA, D, gla_in_w), D ** -0.5),
        'gla_w_gk_up': nrm((N_GLA, GLA_GATE_RANK, GLA_KEY_DIM), GLA_GATE_RANK ** -0.5),
        'gla_b_gk': nrm((N_GLA, GLA_KEY_DIM), 0.1),
        'gla_gn': gain((N_GLA, GLA_DV)),
        'gla_w_out': nrm((N_GLA, GLA_VAL_DIM, D), GLA_VAL_DIM ** -0.5),
        'hgrn_w_in': nrm((N_HGRN, D, hgrn_in_w), D ** -0.5),
        'hgrn_gn': gain((N_HGRN, HGRN_DI)),
        'hgrn_w_out': nrm((N_HGRN, HGRN_I_DIM, D), HGRN_I_DIM ** -0.5),
        'hgrn_lower_bounds': nrm((DEPTH, HGRN_F_DIM), 0.1),
        'ffn2_norm': gain((DEPTH, D)),
        'ffn2_w_gu': nrm((DEPTH, D, 2 * D_FF), D ** -0.5),
        'ffn2_w_down': nrm((DEPTH, D_FF, D), D_FF ** -0.5),
        'ple_norm': gain((DEPTH, D)),
        'ple_w_gate': nrm((DEPTH, D, D), D ** -0.5),
        'ple_w_proj': nrm((DEPTH, PLE_DIM, D), PLE_DIM ** -0.5),
        'final_norm': gain((D,)),
    }


def reference(x_prompt, x_sample, state_gla, state_hgrn, p_prompt, p_sample,
              ffn1_norm, ffn1_w_gu, ffn1_w_down, mix_norm,
              gla_w_in, gla_w_gk_up, gla_b_gk, gla_gn, gla_w_out,
              hgrn_w_in, hgrn_gn, hgrn_w_out, hgrn_lower_bounds,
              ffn2_norm, ffn2_w_gu, ffn2_w_down, ple_norm, ple_w_gate, ple_w_proj, final_norm):
    weights = (ffn1_norm, ffn1_w_gu, ffn1_w_down, mix_norm,
               gla_w_in, gla_w_gk_up, gla_b_gk, gla_gn, gla_w_out,
               hgrn_w_in, hgrn_gn, hgrn_w_out, hgrn_lower_bounds,
               ffn2_norm, ffn2_w_gu, ffn2_w_down, ple_norm, ple_w_gate, ple_w_proj, final_norm)
    bp = x_prompt.shape[0]
    zero_gla = jnp.zeros((N_GLA, bp, GLA_HEADS, GLA_DK, GLA_DV), jnp.float32)
    zero_hgrn = jnp.zeros((N_HGRN, bp, HGRN_HEADS, HGRN_DF, HGRN_DI), jnp.float32)
    y_prompt, gla_prompt, hgrn_prompt = trunk(x_prompt, p_prompt, zero_gla, zero_hgrn, *weights)
    y_sample, gla_sample, hgrn_sample = trunk(x_sample, p_sample, state_gla, state_hgrn, *weights)
    return (y_prompt, y_sample, gla_prompt, gla_sample, hgrn_prompt, hgrn_sample)
```

```python
import functools

import jax
import jax.numpy as jnp
from jax import lax
from jax.experimental import pallas as pl
from jax.experimental.pallas import tpu as pltpu

F32 = jnp.float32
BF16 = jnp.bfloat16
RMS_EPS = 1e-6
GLA_GATE_NORM = 16.0
LANES = 128
FF_CHUNK = 256
ROW_TILE = 512
REC_CHUNK = 64
REC_SUB = 16
REC_SEQS = 8
MASKED_EXPONENT = -1e30
VMEM_LIMIT = 56 * 1024 * 1024


def _cparams(n_axes):
    return pltpu.CompilerParams(dimension_semantics=("arbitrary",) * n_axes,
                                vmem_limit_bytes=VMEM_LIMIT)


def _resident(shape):
    nd = len(shape)
    return pl.BlockSpec(shape, lambda *_: (0,) * nd, pipeline_mode=pl.Buffered(1))


def _rows(tm, width):
    return pl.BlockSpec((tm, width), lambda i: (i, 0))


def _dot(a, b):
    return jnp.dot(a, b, preferred_element_type=F32)


def _rms(x, w):
    ms = jnp.mean(x * x, axis=-1, keepdims=True)
    return x * lax.rsqrt(ms + RMS_EPS) * w


def _sigmoid(x):
    return 1.0 / (1.0 + jnp.exp(-x))


def _silu(x):
    return x * _sigmoid(x)


def _ffn_kernel(x_ref, nw_ref, wg_ref, wu_ref, wd_ref, o_ref, xn_ref, acc_ref):
    x = x_ref[...]
    xn_ref[...] = _rms(x, nw_ref[...]).astype(BF16)
    acc_ref[...] = jnp.zeros_like(acc_ref)

    def slab(c, carry):
        xn = xn_ref[...]
        g = _dot(xn, wg_ref[c])
        u = _dot(xn, wu_ref[c])
        a = (_silu(g) * u).astype(BF16)
        acc_ref[...] += _dot(a, wd_ref[c])
        return carry

    lax.fori_loop(0, wg_ref.shape[0], slab, 0)
    o_ref[...] = x + 0.5 * acc_ref[...]


def _ffn(h, nw, wg3, wu3, wd3, tm):
    n, d = h.shape
    return pl.pallas_call(
        _ffn_kernel,
        grid=(n // tm,),
        in_specs=[_rows(tm, d), _resident(nw.shape), _resident(wg3.shape),
                  _resident(wu3.shape), _resident(wd3.shape)],
        out_specs=_rows(tm, d),
        out_shape=jax.ShapeDtypeStruct((n, d), F32),
        scratch_shapes=[pltpu.VMEM((tm, d), BF16), pltpu.VMEM((tm, d), F32)],
        compiler_params=_cparams(1),
        name="ffn",
    )(h, nw, wg3, wu3, wd3)


def _gla_in_kernel(h_ref, nw_ref, w_ref, wr_ref, wup_ref, bgk_ref,
                   q_ref, k_ref, v_ref, g_ref, og_ref, *, kd, vd):
    u = _rms(h_ref[...], nw_ref[...]).astype(BF16)
    q_ref[...] = _dot(u, w_ref[:, 0:kd])
    k_ref[...] = _dot(u, w_ref[:, kd:2 * kd])
    v_ref[...] = _dot(u, w_ref[:, 2 * kd:2 * kd + vd])
    og_ref[...] = _dot(u, w_ref[:, 2 * kd + vd:2 * kd + 2 * vd])
    r = _dot(u, wr_ref[...]).astype(BF16)
    x = _dot(r, wup_ref[...]) + bgk_ref[...]
    g_ref[...] = (jnp.minimum(x, 0.0) - jnp.log1p(jnp.exp(-jnp.abs(x)))) * (1.0 / GLA_GATE_NORM)


def _gla_in(h, nw, w, wr, wup, bgk, kd, vd, tm):
    n, d = h.shape
    outs = [(n, kd), (n, kd), (n, vd), (n, kd), (n, vd)]
    return pl.pallas_call(
        functools.partial(_gla_in_kernel, kd=kd, vd=vd),
        grid=(n // tm,),
        in_specs=[_rows(tm, d)] + [_resident(a.shape) for a in (nw, w, wr, wup, bgk)],
        out_specs=[_rows(tm, s[1]) for s in outs],
        out_shape=[jax.ShapeDtypeStruct(s, F32) for s in outs],
        compiler_params=_cparams(1),
        name="gla_in",
    )(h, nw, w, wr, wup, bgk)


def _hgrn_in_kernel(h_ref, nw_ref, w_ref, lbw_ref, q_ref, k_ref, v_ref, g_ref, og_ref, *, fd, layer):
    u = _rms(h_ref[...], nw_ref[...]).astype(BF16)
    lbw = lbw_ref[...]
    rows = [lbw[i:i + 1, :] for i in range(lbw.shape[0])]
    m = functools.reduce(jnp.maximum, rows)
    es = [jnp.exp(r - m) for r in rows]
    tot = functools.reduce(jnp.add, es)
    sm = [e / tot for e in es]
    lb = functools.reduce(jnp.add, sm[:layer + 1]) - sm[0]

    q_ref[...] = _silu(_dot(u, w_ref[:, 0:fd]))
    f = _dot(u, w_ref[:, fd:2 * fd])
    e = jnp.exp(-jnp.abs(f))
    inv = 1.0 / (1.0 + e)
    pos = f >= 0.0
    sig_f = jnp.where(pos, inv, e * inv)
    sig_nf = jnp.where(pos, e * inv, inv)
    g_ref[...] = jnp.log(lb + (1.0 - lb) * sig_f)
    k_ref[...] = (1.0 - lb) * sig_nf
    v_ref[...] = _dot(u, w_ref[:, 2 * fd:3 * fd])
    og_ref[...] = _dot(u, w_ref[:, 3 * fd:4 * fd])


def _hgrn_in(h, nw, w, lbw, fd, layer, tm):
    n, d = h.shape
    return pl.pallas_call(
        functools.partial(_hgrn_in_kernel, fd=fd, layer=layer),
        grid=(n // tm,),
        in_specs=[_rows(tm, d)] + [_resident(a.shape) for a in (nw, w, lbw)],
        out_specs=[_rows(tm, fd)] * 5,
        out_shape=[jax.ShapeDtypeStruct((n, fd), F32)] * 5,
        compiler_params=_cparams(1),
        name="hgrn_in",
    )(h, nw, w, lbw)


def _split3(x):
    x1 = x.astype(BF16)
    r1 = x - x1.astype(F32)
    x2 = r1.astype(BF16)
    x3 = (r1 - x2.astype(F32)).astype(BF16)
    return x1, x2, x3


def _rec_kernel(*refs, heads, kdim, vdim, chunk, sub, scale, has_s0):
    if has_s0:
        q_ref, k_ref, v_ref, g_ref, s0_ref, o_ref, s_ref, st_ref = refs
    else:
        q_ref, k_ref, v_ref, g_ref, o_ref, s_ref, st_ref = refs
    n = pl.program_id(1)
    n_seq = q_ref.shape[0]
    c, sb = chunk, sub
    n_sub = c // sb

    @pl.when(n == 0)
    def _init():
        if has_s0:
            def load(s, carry):
                for h in range(heads):
                    st_ref[s, h] = s0_ref[s, h].T
                return carry
            lax.fori_loop(0, n_seq, load, 0)
        else:
            st_ref[...] = jnp.zeros_like(st_ref)

    row = lax.broadcasted_iota(jnp.int32, (c, c), 0)
    col = lax.broadcasted_iota(jnp.int32, (c, c), 1)
    causal = row >= col
    tri = causal.astype(F32).astype(BF16)
    krow = lax.broadcasted_iota(jnp.int32, (c, kdim), 0)

    def seq(s, carry):
        q = q_ref[s]
        k = k_ref[s]
        g = g_ref[s]
        vb = v_ref[s].astype(BF16)
        g1, g2, g3 = _split3(g)
        b = _dot(tri, g1) + _dot(tri, g2) + _dot(tri, g3)
        b_last = b[c - 1:c, :]
        qhat = q * jnp.exp(b) * scale
        kdec = (k * jnp.exp(b_last - b)).astype(BF16)
        d_row = jnp.exp(b_last)
        for h in range(heads):
            ks = slice(h * kdim, (h + 1) * kdim)
            vs = slice(h * vdim, (h + 1) * vdim)
            qh, kh, bh = q[:, ks], k[:, ks], b[:, ks]
            parts = []
            for blk in range(n_sub):
                rs = slice(blk * sb, (blk + 1) * sb)
                if blk == 0:
                    q_blk = qhat[rs, ks]
                    expo = -bh
                else:
                    r_blk = bh[blk * sb - 1:blk * sb, :]
                    q_blk = qh[rs] * jnp.exp(bh[rs] - r_blk) * scale
                    expo = r_blk - bh
                if blk + 1 < n_sub:
                    expo = jnp.where(krow < (blk + 1) * sb, expo, MASKED_EXPONENT)
                k_blk = (kh * jnp.exp(expo)).astype(BF16)
                parts.append(lax.dot_general(q_blk.astype(BF16), k_blk, (((1,), (1,)), ((), ())),
                                             preferred_element_type=F32))
            attn = parts[0] if n_sub == 1 else jnp.concatenate(parts, axis=0)
            attn = jnp.where(causal, attn, 0.0).astype(BF16)
            st = st_ref[s, h]
            o_inter = lax.dot_general(qhat[:, ks].astype(BF16), st.astype(BF16),
                                      (((1,), (1,)), ((), ())), preferred_element_type=F32)
            o_ref[s, :, vs] = _dot(attn, vb[:, vs]) + o_inter
            upd = lax.dot_general(vb[:, vs], kdec[:, ks], (((0,), (0,)), ((), ())),
                                  preferred_element_type=F32)
            st_ref[s, h] = st * d_row[:, ks] + upd
        return carry

    lax.fori_loop(0, n_seq, seq, 0)

    @pl.when(n == pl.num_programs(1) - 1)
    def _final():
        def store(s, carry):
            for h in range(heads):
                s_ref[s, h] = st_ref[s, h].T
            return carry
        lax.fori_loop(0, n_seq, store, 0)


def _rec(q, k, v, g, s0, heads, chunk, sub, scale):
    n_seq, t, hk = q.shape
    hv = v.shape[-1]
    kdim, vdim = hk // heads, hv // heads
    bb = min(REC_SEQS, n_seq)
    grid = (n_seq // bb, t // chunk)
    tok = lambda w: pl.BlockSpec((bb, chunk, w), lambda i, j: (i, j, 0))
    state = pl.BlockSpec((bb, heads, kdim, vdim), lambda i, j: (i, 0, 0, 0))
    in_specs = [tok(hk), tok(hk), tok(hv), tok(hk)]
    args = [q, k, v, g]
    if s0 is not None:
        in_specs.append(state)
        args.append(s0)
    return pl.pallas_call(
        functools.partial(_rec_kernel, heads=heads, kdim=kdim, vdim=vdim, chunk=chunk, sub=sub,
                          scale=scale, has_s0=s0 is not None),
        grid=grid,
        in_specs=in_specs,
        out_specs=[tok(hv), state],
        out_shape=[jax.ShapeDtypeStruct((n_seq, t, hv), F32),
                   jax.ShapeDtypeStruct((n_seq, heads, kdim, vdim), F32)],
        scratch_shapes=[pltpu.VMEM((bb, heads, vdim, kdim), F32)],
        compiler_params=_cparams(2),
        name="rec",
    )(*args)


def _mix_out_kernel(o_ref, og_ref, h_ref, gn_ref, w_ref, out_ref, *, heads, vdim):
    o = o_ref[...]
    gn = gn_ref[...]
    normed = []
    for h in range(heads):
        oh = o[:, h * vdim:(h + 1) * vdim]
        normed.append(_rms(oh, gn))
    y = jnp.concatenate(normed, axis=1) * _silu(og_ref[...])
    out_ref[...] = h_ref[...] + _dot(y.astype(BF16), w_ref[...])


def _mix_out(o, og, h, gn, w, heads, tm):
    n, hv = o.shape
    d = h.shape[1]
    return pl.pallas_call(
        functools.partial(_mix_out_kernel, heads=heads, vdim=hv // heads),
        grid=(n // tm,),
        in_specs=[_rows(tm, hv), _rows(tm, hv), _rows(tm, d), _resident(gn.shape), _resident(w.shape)],
        out_specs=_rows(tm, d),
        out_shape=jax.ShapeDtypeStruct((n, d), F32),
        compiler_params=_cparams(1),
        name="mix_out",
    )(o, og, h, gn, w)


def _ple_kernel(h_ref, p_ref, nw_ref, wg_ref, wp_ref, fn_ref, out_ref, *, final):
    h = h_ref[...]
    gate = _sigmoid(_dot(_rms(h, nw_ref[...]).astype(BF16), wg_ref[...]))
    hn = h + gate * _dot(p_ref[...].astype(BF16), wp_ref[...])
    if final:
        hn = _rms(hn, fn_ref[...])
    out_ref[...] = hn


def _ple(h, p, nw, wg, wp, fn, final, tm):
    n, d = h.shape
    return pl.pallas_call(
        functools.partial(_ple_kernel, final=final),
        grid=(n // tm,),
        in_specs=[_rows(tm, d), _rows(tm, p.shape[1])] + [_resident(a.shape) for a in (nw, wg, wp, fn)],
        out_specs=_rows(tm, d),
        out_shape=jax.ShapeDtypeStruct((n, d), F32),
        compiler_params=_cparams(1),
        name="ple",
    )(h, p, nw, wg, wp, fn)


def _prep_weights(ffn1_norm, ffn1_w_gu, ffn1_w_down, mix_norm, gla_w_in, gla_w_gk_up, gla_b_gk, gla_gn,
                  gla_w_out, hgrn_w_in, hgrn_gn, hgrn_w_out, hgrn_lower_bounds, ffn2_norm, ffn2_w_gu,
                  ffn2_w_down, ple_norm, ple_w_gate, ple_w_proj, final_norm):
    depth, d, two_ff = ffn1_w_gu.shape
    d_ff = two_ff // 2
    n_slab = d_ff // FF_CHUNK

    def ffn_w(w_gu, w_down):
        slabs = lambda w: w.reshape(depth, d, n_slab, FF_CHUNK).transpose(0, 2, 1, 3).astype(BF16)
        return (slabs(w_gu[:, :, :d_ff]), slabs(w_gu[:, :, d_ff:]),
                w_down.reshape(depth, n_slab, FF_CHUNK, d).astype(BF16))

    row = lambda a: a.reshape(a.shape[0], 1, a.shape[1])
    kd = gla_w_gk_up.shape[2]
    rank = gla_w_gk_up.shape[1]
    main = gla_w_in.shape[2] - rank
    return dict(
        ffn1=ffn_w(ffn1_w_gu, ffn1_w_down), ffn2=ffn_w(ffn2_w_gu, ffn2_w_down),
        ffn1_norm=row(ffn1_norm), ffn2_norm=row(ffn2_norm), mix_norm=row(mix_norm),
        ple_norm=row(ple_norm), final_norm=final_norm.reshape(1, -1),
        gla_w=gla_w_in[:, :, :main].astype(BF16),
        gla_wr=jnp.pad(gla_w_in[:, :, main:], ((0, 0), (0, 0), (0, LANES - rank))).astype(BF16),
        gla_wup=jnp.pad(gla_w_gk_up, ((0, 0), (0, LANES - rank), (0, 0))).astype(BF16),
        gla_bgk=row(gla_b_gk), gla_gn=row(gla_gn), gla_w_out=gla_w_out.astype(BF16), gla_kd=kd,
        hgrn_w=hgrn_w_in.astype(BF16), hgrn_gn=row(hgrn_gn), hgrn_w_out=hgrn_w_out.astype(BF16),
        hgrn_lb=hgrn_lower_bounds,
        ple_wg=ple_w_gate.astype(BF16), ple_wp=ple_w_proj.astype(BF16),
    )


def _pad_time(a, t_pad):
    return jnp.pad(a, ((0, 0), (0, t_pad - a.shape[1]), (0, 0)))


def _trunk(x, p, st_gla, st_hgrn, w, gla_heads, hgrn_heads):
    bsz, t, d = x.shape
    n = bsz * t
    depth = p.shape[0]
    tm = min(ROW_TILE, n)
    chunk = REC_CHUNK if t >= REC_CHUNK else REC_SUB
    t_pad = -(-t // chunk) * chunk
    h = x.reshape(n, d)
    new_gla, new_hgrn = [], []
    for li in range(depth):
        h = _ffn(h, w["ffn1_norm"][li], *(a[li] for a in w["ffn1"]), tm)
        j = li // 2
        if li % 2 == 0:
            kd = w["gla_kd"]
            vd = w["gla_w_out"].shape[1]
            heads = gla_heads
            q, k, v, g, og = _gla_in(h, w["mix_norm"][li], w["gla_w"][j], w["gla_wr"][j], w["gla_wup"][j],
                                     w["gla_bgk"][j], kd, vd, tm)
            s0 = None if st_gla is None else st_gla[j]
            scale = (kd // heads) ** -0.5
            gn, w_out = w["gla_gn"][j], w["gla_w_out"][j]
        else:
            fd = w["hgrn_w_out"].shape[1]
            heads = hgrn_heads
            q, k, v, g, og = _hgrn_in(h, w["mix_norm"][li], w["hgrn_w"][j], w["hgrn_lb"], fd, li, tm)
            s0 = None if st_hgrn is None else st_hgrn[j]
            scale = 1.0
            gn, w_out = w["hgrn_gn"][j], w["hgrn_w_out"][j]
        seqs = lambda a: _pad_time(a.reshape(bsz, t, a.shape[1]), t_pad)
        o, s = _rec(seqs(q), seqs(k), seqs(v), seqs(g), s0, heads, chunk, min(REC_SUB, chunk), scale)
        (new_gla if li % 2 == 0 else new_hgrn).append(s)
        o = o[:, :t].reshape(n, o.shape[-1])
        h = _mix_out(o, og, h, gn, w_out, heads, tm)
        h = _ffn(h, w["ffn2_norm"][li], *(a[li] for a in w["ffn2"]), tm)
        h = _ple(h, p[li].reshape(n, p.shape[-1]), w["ple_norm"][li], w["ple_wg"][li], w["ple_wp"][li],
                 w["final_norm"], li == depth - 1, tm)
    return h.reshape(bsz, t, d), jnp.stack(new_gla), jnp.stack(new_hgrn)


def kernel(x_prompt, x_sample, state_gla, state_hgrn, p_prompt, p_sample, ffn1_norm, ffn1_w_gu, ffn1_w_down, mix_norm, gla_w_in, gla_w_gk_up, gla_b_gk, gla_gn, gla_w_out, hgrn_w_in, hgrn_gn, hgrn_w_out, hgrn_lower_bounds, ffn2_norm, ffn2_w_gu, ffn2_w_down, ple_norm, ple_w_gate, ple_w_proj, final_norm):
    w = _prep_weights(ffn1_norm, ffn1_w_gu, ffn1_w_down, mix_norm, gla_w_in, gla_w_gk_up, gla_b_gk, gla_gn,
                      gla_w_out, hgrn_w_in, hgrn_gn, hgrn_w_out, hgrn_lower_bounds, ffn2_norm, ffn2_w_gu,
                      ffn2_w_down, ple_norm, ple_w_gate, ple_w_proj, final_norm)
    gla_heads = state_gla.shape[2]
    hgrn_heads = state_hgrn.shape[2]
    y_p, gla_p, hgrn_p = _trunk(x_prompt, p_prompt, None, None, w, gla_heads, hgrn_heads)
    y_s, gla_s, hgrn_s = _trunk(x_sample, p_sample, state_gla, state_hgrn, w, gla_heads, hgrn_heads)
    return (y_p, y_s, gla_p, gla_s, hgrn_p, hgrn_s)
```

```python
import functools

import jax
import jax.numpy as jnp
from jax import lax
from jax.experimental import pallas as pl
from jax.experimental.pallas import tpu as pltpu

F32 = jnp.float32
BF16 = jnp.bfloat16
RMS_EPS = 1e-6
GLA_GATE_NORM = 16.0
LANES = 128
BF16_ROWS = 16
FF_CHUNK = 256
ROW_TILE = 512
REC_CHUNK = 64
REC_SUB = 16
REC_SEQS = 8
VMEM_LIMIT = 56 * 1024 * 1024

_NT = (((1,), (1,)), ((), ()))
_TN = (((0,), (0,)), ((), ()))


def _cparams(n_axes):
    return pltpu.CompilerParams(dimension_semantics=("arbitrary",) * n_axes,
                                vmem_limit_bytes=VMEM_LIMIT)


def _resident(shape):
    nd = len(shape)
    return pl.BlockSpec(shape, lambda *_: (0,) * nd, pipeline_mode=pl.Buffered(1))


def _rows(tm, width):
    return pl.BlockSpec((tm, width), lambda i: (i, 0))


def _dot(a, b, dims=None):
    if dims is None:
        return jnp.dot(a, b, preferred_element_type=F32)
    return lax.dot_general(a, b, dims, preferred_element_type=F32)


def _rms(x, w):
    ms = jnp.mean(x * x, axis=-1, keepdims=True)
    return x * lax.rsqrt(ms + RMS_EPS) * w


def _sigmoid(x):
    return 1.0 / (1.0 + jnp.exp(-x))


def _silu(x):
    return x * _sigmoid(x)


def _ffn_kernel(x_ref, nw_ref, wgu_ref, wd_ref, o_ref, xn_ref, acc_ref, *, d_ff):
    x = x_ref[...]
    xn_ref[...] = _rms(x, nw_ref[...]).astype(BF16)
    for c in range(d_ff // FF_CHUNK):
        lo, hi = c * FF_CHUNK, (c + 1) * FF_CHUNK
        xn = xn_ref[...]
        g = _dot(xn, wgu_ref[:, lo:hi])
        u = _dot(xn, wgu_ref[:, d_ff + lo:d_ff + hi])
        part = _dot((_silu(g) * u).astype(BF16), wd_ref[lo:hi, :])
        if c == 0:
            acc_ref[...] = part
        else:
            acc_ref[...] += part
    o_ref[...] = x + 0.5 * acc_ref[...]


def _ffn(h, nw, wgu, wd, tm):
    n, d = h.shape
    return pl.pallas_call(
        functools.partial(_ffn_kernel, d_ff=wd.shape[0]),
        grid=(n // tm,),
        in_specs=[_rows(tm, d), _resident(nw.shape), _resident(wgu.shape), _resident(wd.shape)],
        out_specs=_rows(tm, d),
        out_shape=jax.ShapeDtypeStruct((n, d), F32),
        scratch_shapes=[pltpu.VMEM((tm, d), BF16), pltpu.VMEM((tm, d), F32)],
        compiler_params=_cparams(1),
        name="ffn",
    )(h, nw, wgu, wd)


def _gla_in_kernel(h_ref, nw_ref, w_ref, wr_ref, wup_ref, bgk_ref,
                   q_ref, k_ref, v_ref, g_ref, og_ref, *, kd, vd):
    u = _rms(h_ref[...], nw_ref[...]).astype(BF16)
    q_ref[...] = _dot(u, w_ref[:, 0:kd])
    k_ref[...] = _dot(u, w_ref[:, kd:2 * kd])
    v_ref[...] = _dot(u, w_ref[:, 2 * kd:2 * kd + vd])
    og_ref[...] = _dot(u, w_ref[:, 2 * kd + vd:2 * kd + 2 * vd])
    r = _dot(u, wr_ref[...]).astype(BF16)
    x = _dot(r, wup_ref[...]) + bgk_ref[...]
    g_ref[...] = (jnp.minimum(x, 0.0) - jnp.log1p(jnp.exp(-jnp.abs(x)))) * (1.0 / GLA_GATE_NORM)


def _gla_in(h, nw, w, wr, wup, bgk, kd, vd, tm):
    n, d = h.shape
    outs = [(n, kd), (n, kd), (n, vd), (n, kd), (n, vd)]
    return pl.pallas_call(
        functools.partial(_gla_in_kernel, kd=kd, vd=vd),
        grid=(n // tm,),
        in_specs=[_rows(tm, d)] + [_resident(a.shape) for a in (nw, w, wr, wup, bgk)],
        out_specs=[_rows(tm, s[1]) for s in outs],
        out_shape=[jax.ShapeDtypeStruct(s, F32) for s in outs],
        compiler_params=_cparams(1),
        name="gla_in",
    )(h, nw, w, wr, wup, bgk)


def _hgrn_in_kernel(h_ref, nw_ref, w_ref, lbw_ref, q_ref, k_ref, v_ref, g_ref, og_ref, *, fd, layer):
    u = _rms(h_ref[...], nw_ref[...]).astype(BF16)
    lbw = lbw_ref[...]
    rows = [lbw[i:i + 1, :] for i in range(lbw.shape[0])]
    m = functools.reduce(jnp.maximum, rows)
    es = [jnp.exp(r - m) for r in rows]
    tot = functools.reduce(jnp.add, es)
    sm = [e / tot for e in es]
    lb = functools.reduce(jnp.add, sm[:layer + 1]) - sm[0]

    q_ref[...] = _silu(_dot(u, w_ref[:, 0:fd]))
    f = _dot(u, w_ref[:, fd:2 * fd])
    e = jnp.exp(-jnp.abs(f))
    inv = 1.0 / (1.0 + e)
    pos = f >= 0.0
    sig_f = jnp.where(pos, inv, e * inv)
    sig_nf = jnp.where(pos, e * inv, inv)
    g_ref[...] = jnp.log(lb + (1.0 - lb) * sig_f)
    k_ref[...] = (1.0 - lb) * sig_nf
    v_ref[...] = _dot(u, w_ref[:, 2 * fd:3 * fd])
    og_ref[...] = _dot(u, w_ref[:, 3 * fd:4 * fd])


def _hgrn_in(h, nw, w, lbw, fd, layer, tm):
    n, d = h.shape
    return pl.pallas_call(
        functools.partial(_hgrn_in_kernel, fd=fd, layer=layer),
        grid=(n // tm,),
        in_specs=[_rows(tm, d)] + [_resident(a.shape) for a in (nw, w, lbw)],
        out_specs=[_rows(tm, fd)] * 5,
        out_shape=[jax.ShapeDtypeStruct((n, fd), F32)] * 5,
        compiler_params=_cparams(1),
        name="hgrn_in",
    )(h, nw, w, lbw)


def _split3(x):
    x1 = x.astype(BF16)
    r1 = x - x1.astype(F32)
    x2 = r1.astype(BF16)
    x3 = (r1 - x2.astype(F32)).astype(BF16)
    return x1, x2, x3


def _rec_kernel(*refs, heads, kdim, vdim, chunk, sub, scale, has_s0, single):
    refs = list(refs)
    q_ref, k_ref, v_ref, g_ref = refs[:4]
    s0_ref = refs[4] if has_s0 else None
    o_ref, s_ref = refs[4 + has_s0:6 + has_s0]
    scratch = refs[6 + has_s0:]
    c, sb = chunk, sub
    n_sub = c // sb
    if not single:
        st_ref, scratch = scratch[0], scratch[1:]
    if n_sub > 1:
        qx_s, scratch = scratch[0], scratch[1:]
    qhat_s, kdec_s, kvar_s, vb_s, dec_s, attn_s = scratch
    if n_sub == 1:
        qx_s = qhat_s
    n = pl.program_id(1)
    n_seq = q_ref.shape[0]
    hk = heads * kdim

    if not single:
        @pl.when(n == 0)
        def _init():
            if has_s0:
                def load(s, carry):
                    for h in range(heads):
                        st_ref[s, h] = s0_ref[s, h].T
                    return carry
                lax.fori_loop(0, n_seq, load, 0)
            else:
                st_ref[...] = jnp.zeros_like(st_ref)

    causal = (lax.broadcasted_iota(jnp.int32, (c, c), 0) >= lax.broadcasted_iota(jnp.int32, (c, c), 1))
    tri = causal.astype(F32).astype(BF16)

    def prepare(s, carry):
        q = q_ref[s]
        k = k_ref[s]
        g1, g2, g3 = _split3(g_ref[s])
        b = _dot(tri, g1) + _dot(tri, g2) + _dot(tri, g3)
        b_last = b[c - 1:c, :]
        qhat = q * jnp.exp(b) * scale
        qhat_s[s] = qhat.astype(BF16)
        kdec_s[s] = (k * jnp.exp(b_last - b)).astype(BF16)
        vb_s[s] = v_ref[s].astype(BF16)
        if single:
            p1, p2, p3 = (p.astype(F32) for p in _split3(b_last))
            r = lax.broadcasted_iota(jnp.int32, (BF16_ROWS, hk), 0)
            dec_s[s] = jnp.where(r == 0, p1, jnp.where(r == 1, p2, jnp.where(r == 2, p3, 0.0))).astype(BF16)
        else:
            dec_s[s] = jnp.exp(b_last)
        qx = [qhat[0:sb]]
        for blk in range(n_sub):
            rows = (blk + 1) * sb
            if blk == 0:
                expo = -b[:rows]
            else:
                r_blk = b[blk * sb - 1:blk * sb, :]
                expo = r_blk - b[:rows]
                rs = slice(blk * sb, rows)
                qx.append(q[rs] * jnp.exp(b[rs] - r_blk) * scale)
            kv = (k[:rows] * jnp.exp(expo)).astype(BF16)
            if rows < c:
                kv = jnp.concatenate([kv, jnp.zeros((c - rows, hk), BF16)], axis=0)
            kvar_s[s, blk] = kv
        if n_sub > 1:
            qx_s[s] = jnp.concatenate(qx, axis=0).astype(BF16)
        return carry

    def scores(s, carry):
        for h in range(heads):
            ks = slice(h * kdim, (h + 1) * kdim)
            parts = [_dot(qx_s[s, blk * sb:(blk + 1) * sb, ks], kvar_s[s, blk, :, ks], _NT)
                     for blk in range(n_sub)]
            attn = parts[0] if n_sub == 1 else jnp.concatenate(parts, axis=0)
            attn_s[s, h] = jnp.where(causal, attn, 0.0).astype(BF16)
        return carry

    def update(s, carry):
        for h in range(heads):
            ks = slice(h * kdim, (h + 1) * kdim)
            vs = slice(h * vdim, (h + 1) * vdim)
            vh = vb_s[s, :, vs]
            o_intra = _dot(attn_s[s, h], vh)
            if single:
                st = s0_ref[s, h] if has_s0 else jnp.zeros((kdim, vdim), F32)
                o_ref[s, :, vs] = o_intra + _dot(qhat_s[s, :, ks], st.astype(BF16))
                ones = jnp.ones((BF16_ROWS, LANES), BF16)
                dec = jnp.exp(_dot(dec_s[s, :, ks], ones, _TN))
                dec = jnp.concatenate([dec] * (vdim // LANES), axis=1)
                s_ref[s, h] = st * dec + _dot(kdec_s[s, :, ks], vh, _TN)
            else:
                st = st_ref[s, h]
                o_ref[s, :, vs] = o_intra + _dot(qhat_s[s, :, ks], st.astype(BF16), _NT)
                st_ref[s, h] = st * dec_s[s, :, ks] + _dot(vh, kdec_s[s, :, ks], _TN)
        return carry

    for s in range(n_seq):
        prepare(s, 0)
        scores(s, 0)
        update(s, 0)

    if not single:
        @pl.when(n == pl.num_programs(1) - 1)
        def _final():
            def store(s, carry):
                for h in range(heads):
                    s_ref[s, h] = st_ref[s, h].T
                return carry
            lax.fori_loop(0, n_seq, store, 0)


def _rec(q, k, v, g, s0, heads, chunk, sub, scale):
    n_seq, t, hk = q.shape
    hv = v.shape[-1]
    kdim, vdim = hk // heads, hv // heads
    bb = min(REC_SEQS, n_seq)
    n_sub = chunk // sub
    single = t == chunk
    grid = (n_seq // bb, t // chunk)
    tok = lambda w: pl.BlockSpec((bb, chunk, w), lambda i, j: (i, j, 0))
    state = pl.BlockSpec((bb, heads, kdim, vdim), lambda i, j: (i, 0, 0, 0))
    in_specs = [tok(hk), tok(hk), tok(hv), tok(hk)]
    args = [q, k, v, g]
    if s0 is not None:
        in_specs.append(state)
        args.append(s0)
    scratch = []
    if not single:
        scratch.append(pltpu.VMEM((bb, heads, vdim, kdim), F32))
    if n_sub > 1:
        scratch.append(pltpu.VMEM((bb, chunk, hk), BF16))
    scratch += [
        pltpu.VMEM((bb, chunk, hk), BF16),
        pltpu.VMEM((bb, chunk, hk), BF16),
        pltpu.VMEM((bb, n_sub, chunk, hk), BF16),
        pltpu.VMEM((bb, chunk, hv), BF16),
        pltpu.VMEM((bb, BF16_ROWS, hk), BF16) if single else pltpu.VMEM((bb, 1, hk), F32),
        pltpu.VMEM((bb, heads, chunk, chunk), BF16),
    ]
    return pl.pallas_call(
        functools.partial(_rec_kernel, heads=heads, kdim=kdim, vdim=vdim, chunk=chunk, sub=sub,
                          scale=scale, has_s0=s0 is not None, single=single),
        grid=grid,
        in_specs=in_specs,
        out_specs=[tok(hv), state],
        out_shape=[jax.ShapeDtypeStruct((n_seq, t, hv), F32),
                   jax.ShapeDtypeStruct((n_seq, heads, kdim, vdim), F32)],
        scratch_shapes=scratch,
        compiler_params=_cparams(2),
        name="rec",
    )(*args)


def _mix_out_kernel(o_ref, og_ref, h_ref, gn_ref, w_ref, out_ref, *, heads, vdim):
    o = o_ref[...]
    gn = gn_ref[...]
    normed = []
    for h in range(heads):
        oh = o[:, h * vdim:(h + 1) * vdim]
        normed.append(_rms(oh, gn))
    y = jnp.concatenate(normed, axis=1) * _silu(og_ref[...])
    out_ref[...] = h_ref[...] + _dot(y.astype(BF16), w_ref[...])


def _mix_out(o, og, h, gn, w, heads, tm):
    n, hv = o.shape
    d = h.shape[1]
    return pl.pallas_call(
        functools.partial(_mix_out_kernel, heads=heads, vdim=hv // heads),
        grid=(n // tm,),
        in_specs=[_rows(tm, hv), _rows(tm, hv), _rows(tm, d), _resident(gn.shape), _resident(w.shape)],
        out_specs=_rows(tm, d),
        out_shape=jax.ShapeDtypeStruct((n, d), F32),
        compiler_params=_cparams(1),
        name="mix_out",
    )(o, og, h, gn, w)


def _ple_kernel(h_ref, p_ref, nw_ref, wg_ref, wp_ref, fn_ref, out_ref, *, final):
    h = h_ref[...]
    gate = _sigmoid(_dot(_rms(h, nw_ref[...]).astype(BF16), wg_ref[...]))
    hn = h + gate * _dot(p_ref[...].astype(BF16), wp_ref[...])
    if final:
        hn = _rms(hn, fn_ref[...])
    out_ref[...] = hn


def _ple(h, p, nw, wg, wp, fn, final, tm):
    n, d = h.shape
    return pl.pallas_call(
        functools.partial(_ple_kernel, final=final),
        grid=(n // tm,),
        in_specs=[_rows(tm, d), _rows(tm, p.shape[1])] + [_resident(a.shape) for a in (nw, wg, wp, fn)],
        out_specs=_rows(tm, d),
        out_shape=jax.ShapeDtypeStruct((n, d), F32),
        compiler_params=_cparams(1),
        name="ple",
    )(h, p, nw, wg, wp, fn)


def _prep_weights(ffn1_norm, ffn1_w_gu, ffn1_w_down, mix_norm, gla_w_in, gla_w_gk_up, gla_b_gk, gla_gn,
                  gla_w_out, hgrn_w_in, hgrn_gn, hgrn_w_out, hgrn_lower_bounds, ffn2_norm, ffn2_w_gu,
                  ffn2_w_down, ple_norm, ple_w_gate, ple_w_proj, final_norm):
    row = lambda a: a.reshape(a.shape[0], 1, a.shape[1])
    kd = gla_w_gk_up.shape[2]
    rank = gla_w_gk_up.shape[1]
    main = gla_w_in.shape[2] - rank
    return dict(
        ffn1=(ffn1_w_gu.astype(BF16), ffn1_w_down.astype(BF16)),
        ffn2=(ffn2_w_gu.astype(BF16), ffn2_w_down.astype(BF16)),
        ffn1_norm=row(ffn1_norm), ffn2_norm=row(ffn2_norm), mix_norm=row(mix_norm),
        ple_norm=row(ple_norm), final_norm=final_norm.reshape(1, -1),
        gla_w=gla_w_in[:, :, :main].astype(BF16),
        gla_wr=jnp.pad(gla_w_in[:, :, main:], ((0, 0), (0, 0), (0, LANES - rank))).astype(BF16),
        gla_wup=jnp.pad(gla_w_gk_up, ((0, 0), (0, LANES - rank), (0, 0))).astype(BF16),
        gla_bgk=row(gla_b_gk), gla_gn=row(gla_gn), gla_w_out=gla_w_out.astype(BF16), gla_kd=kd,
        hgrn_w=hgrn_w_in.astype(BF16), hgrn_gn=row(hgrn_gn), hgrn_w_out=hgrn_w_out.astype(BF16),
        hgrn_lb=hgrn_lower_bounds,
        ple_wg=ple_w_gate.astype(BF16), ple_wp=ple_w_proj.astype(BF16),
    )


def _trunk(x, p, st_gla, st_hgrn, w, gla_heads, hgrn_heads):
    bsz, t, d = x.shape
    n = bsz * t
    depth = p.shape[0]
    tm = min(ROW_TILE, n)
    chunk = REC_CHUNK if t >= REC_CHUNK else REC_SUB
    t_pad = -(-t // chunk) * chunk
    h = x.reshape(n, d)
    new_gla, new_hgrn = [], []

    def seqs(a):
        a = a.reshape(bsz, t, a.shape[1])
        return a if t_pad == t else jnp.pad(a, ((0, 0), (0, t_pad - t), (0, 0)))

    for li in range(depth):
        h = _ffn(h, w["ffn1_norm"][li], w["ffn1"][0][li], w["ffn1"][1][li], tm)
        j = li // 2
        if li % 2 == 0:
            kd = w["gla_kd"]
            vd = w["gla_w_out"].shape[1]
            heads = gla_heads
            q, k, v, g, og = _gla_in(h, w["mix_norm"][li], w["gla_w"][j], w["gla_wr"][j], w["gla_wup"][j],
                                     w["gla_bgk"][j], kd, vd, tm)
            s0 = None if st_gla is None else st_gla[j]
            scale = (kd // heads) ** -0.5
            gn, w_out = w["gla_gn"][j], w["gla_w_out"][j]
        else:
            fd = w["hgrn_w_out"].shape[1]
            heads = hgrn_heads
            q, k, v, g, og = _hgrn_in(h, w["mix_norm"][li], w["hgrn_w"][j], w["hgrn_lb"], fd, li, tm)
            s0 = None if st_hgrn is None else st_hgrn[j]
            scale = 1.0
            gn, w_out = w["hgrn_gn"][j], w["hgrn_w_out"][j]
        o, s = _rec(seqs(q), seqs(k), seqs(v), seqs(g), s0, heads, chunk, min(REC_SUB, chunk), scale)
        (new_gla if li % 2 == 0 else new_hgrn).append(s)
        if t_pad != t:
            o = o[:, :t]
        h = _mix_out(o.reshape(n, o.shape[-1]), og, h, gn, w_out, heads, tm)
        h = _ffn(h, w["ffn2_norm"][li], w["ffn2"][0][li], w["ffn2"][1][li], tm)
        h = _ple(h, p[li].reshape(n, p.shape[-1]), w["ple_norm"][li], w["ple_wg"][li], w["ple_wp"][li],
                 w["final_norm"], li == depth - 1, tm)
    return h.reshape(bsz, t, d), jnp.stack(new_gla), jnp.stack(new_hgrn)


def kernel(x_prompt, x_sample, state_gla, state_hgrn, p_prompt, p_sample, ffn1_norm, ffn1_w_gu, ffn1_w_down, mix_norm, gla_w_in, gla_w_gk_up, gla_b_gk, gla_gn, gla_w_out, hgrn_w_in, hgrn_gn, hgrn_w_out, hgrn_lower_bounds, ffn2_norm, ffn2_w_gu, ffn2_w_down, ple_norm, ple_w_gate, ple_w_proj, final_norm):
    w = _prep_weights(ffn1_norm, ffn1_w_gu, ffn1_w_down, mix_norm, gla_w_in, gla_w_gk_up, gla_b_gk, gla_gn,
                      gla_w_out, hgrn_w_in, hgrn_gn, hgrn_w_out, hgrn_lower_bounds, ffn2_norm, ffn2_w_gu,
                      ffn2_w_down, ple_norm, ple_w_gate, ple_w_proj, final_norm)
    gla_heads = state_gla.shape[2]
    hgrn_heads = state_hgrn.shape[2]
    y_p, gla_p, hgrn_p = _trunk(x_prompt, p_prompt, None, None, w, gla_heads, hgrn_heads)
    y_s, gla_s, hgrn_s = _trunk(x_sample, p_sample, state_gla, state_hgrn, w, gla_heads, hgrn_heads)
    return (y_p, y_s, gla_p, gla_s, hgrn_p, hgrn_s)
```

```python
import functools

import jax
import jax.numpy as jnp
from jax import lax
from jax.experimental import pallas as pl
from jax.experimental.pallas import tpu as pltpu

F32 = jnp.float32
BF16 = jnp.bfloat16
RMS_EPS = 1e-6
GLA_GATE_NORM = 16.0
LANES = 128
BF16_ROWS = 16
FF_CHUNK = 256
ROW_TILE = 512
REC_CHUNK = 64
REC_SUB = 16
REC_SEQS = 8
VMEM_LIMIT = 56 * 1024 * 1024

_NT = (((1,), (1,)), ((), ()))
_TN = (((0,), (0,)), ((), ()))


def _cparams(n_axes):
    return pltpu.CompilerParams(dimension_semantics=("arbitrary",) * n_axes,
                                vmem_limit_bytes=VMEM_LIMIT)


def _resident(shape):
    nd = len(shape)
    return pl.BlockSpec(shape, lambda *_: (0,) * nd, pipeline_mode=pl.Buffered(1))


def _layer(arr, li):
    nd = arr.ndim - 1
    return pl.BlockSpec((None,) + arr.shape[1:], lambda *_: (li,) + (0,) * nd,
                        pipeline_mode=pl.Buffered(1))


def _rows(tm, width):
    return pl.BlockSpec((tm, width), lambda i: (i, 0))


def _dot(a, b, dims=None):
    if dims is None:
        return jnp.dot(a, b, preferred_element_type=F32)
    return lax.dot_general(a, b, dims, preferred_element_type=F32)


def _rms(x, w):
    ms = jnp.mean(x * x, axis=-1, keepdims=True)
    return x * lax.rsqrt(ms + RMS_EPS) * w


def _sigmoid(x):
    return 1.0 / (1.0 + jnp.exp(-x))


def _silu(x):
    return x * _sigmoid(x)


def _ffn_acc(x, nw_ref, wgu_ref, wd_ref, xn_ref, acc_ref):
    d_ff = wd_ref.shape[0]
    xn_ref[...] = _rms(x, nw_ref[...]).astype(BF16)
    for c in range(d_ff // FF_CHUNK):
        lo, hi = c * FF_CHUNK, (c + 1) * FF_CHUNK
        xn = xn_ref[...]
        g = _dot(xn, wgu_ref[:, lo:hi])
        u = _dot(xn, wgu_ref[:, d_ff + lo:d_ff + hi])
        part = _dot((_silu(g) * u).astype(BF16), wd_ref[lo:hi, :])
        if c == 0:
            acc_ref[...] = part
        else:
            acc_ref[...] += part


def _ffn_kernel(x_ref, nw_ref, wgu_ref, wd_ref, o_ref, xn_ref, acc_ref):
    x = x_ref[...]
    _ffn_acc(x, nw_ref, wgu_ref, wd_ref, xn_ref, acc_ref)
    o_ref[...] = x + 0.5 * acc_ref[...]


def _ffn(h, nw, wgu, wd, li, tm):
    n, d = h.shape
    return pl.pallas_call(
        _ffn_kernel,
        grid=(n // tm,),
        in_specs=[_rows(tm, d), _layer(nw, li), _layer(wgu, li), _layer(wd, li)],
        out_specs=_rows(tm, d),
        out_shape=jax.ShapeDtypeStruct((n, d), F32),
        scratch_shapes=[pltpu.VMEM((tm, d), BF16), pltpu.VMEM((tm, d), F32)],
        compiler_params=_cparams(1),
        name="ffn",
    )(h, nw, wgu, wd)


def _gla_proj(u, w_ref, wr_ref, wup_ref, bgk_ref, kd, vd):
    q = _dot(u, w_ref[:, 0:kd])
    k = _dot(u, w_ref[:, kd:2 * kd])
    v = _dot(u, w_ref[:, 2 * kd:2 * kd + vd])
    og = _dot(u, w_ref[:, 2 * kd + vd:2 * kd + 2 * vd])
    r = _dot(u, wr_ref[...]).astype(BF16)
    x = _dot(r, wup_ref[...]) + bgk_ref[...]
    g = (jnp.minimum(x, 0.0) - jnp.log1p(jnp.exp(-jnp.abs(x)))) * (1.0 / GLA_GATE_NORM)
    return q, k, v, g, og


def _hgrn_proj(u, w_ref, lbw_ref, fd, layer):
    lbw = lbw_ref[...]
    rows = [lbw[i:i + 1, :] for i in range(lbw.shape[0])]
    m = functools.reduce(jnp.maximum, rows)
    es = [jnp.exp(r - m) for r in rows]
    tot = functools.reduce(jnp.add, es)
    sm = [e / tot for e in es]
    lb = functools.reduce(jnp.add, sm[:layer + 1]) - sm[0]

    q = _silu(_dot(u, w_ref[:, 0:fd]))
    f = _dot(u, w_ref[:, fd:2 * fd])
    e = jnp.exp(-jnp.abs(f))
    inv = 1.0 / (1.0 + e)
    pos = f >= 0.0
    sig_f = jnp.where(pos, inv, e * inv)
    sig_nf = jnp.where(pos, e * inv, inv)
    g = jnp.log(lb + (1.0 - lb) * sig_f)
    k = (1.0 - lb) * sig_nf
    v = _dot(u, w_ref[:, 2 * fd:3 * fd])
    og = _dot(u, w_ref[:, 3 * fd:4 * fd])
    return q, k, v, g, og


def _in_proj_kernel(*refs, kind, hk, hv, layer):
    if kind == "gla":
        h_ref, nw_ref, w_ref, wr_ref, wup_ref, bgk_ref, q_ref, k_ref, v_ref, g_ref, og_ref = refs
    else:
        h_ref, nw_ref, w_ref, lbw_ref, q_ref, k_ref, v_ref, g_ref, og_ref = refs
    u = _rms(h_ref[...], nw_ref[...]).astype(BF16)
    if kind == "gla":
        q, k, v, g, og = _gla_proj(u, w_ref, wr_ref, wup_ref, bgk_ref, hk, hv)
    else:
        q, k, v, g, og = _hgrn_proj(u, w_ref, lbw_ref, hk, layer)
    q_ref[...] = q
    k_ref[...] = k
    v_ref[...] = v.astype(BF16)
    g_ref[...] = g
    og_ref[...] = og


def _in_proj(h, kind, params, li, j, tm):
    n, d = h.shape
    if kind == "gla":
        nw, w, wr, wup, bgk = params
        hk, hv = wup.shape[2], (w.shape[2] - 2 * wup.shape[2]) // 2
        in_specs = [_layer(nw, li), _layer(w, j), _layer(wr, j), _layer(wup, j), _layer(bgk, j)]
    else:
        nw, w, lbw = params
        hk = hv = w.shape[2] // 4
        in_specs = [_layer(nw, li), _layer(w, j), _resident(lbw.shape)]
    outs = [((n, hk), F32), ((n, hk), F32), ((n, hv), BF16), ((n, hk), F32), ((n, hv), F32)]
    return pl.pallas_call(
        functools.partial(_in_proj_kernel, kind=kind, hk=hk, hv=hv, layer=li),
        grid=(n // tm,),
        in_specs=[_rows(tm, d)] + in_specs,
        out_specs=[_rows(tm, s[0][1]) for s in outs],
        out_shape=[jax.ShapeDtypeStruct(*s) for s in outs],
        compiler_params=_cparams(1),
        name="in_proj_" + kind,
    )(h, *params)


def _split3(x):
    x1 = x.astype(BF16)
    r1 = x - x1.astype(F32)
    x2 = r1.astype(BF16)
    x3 = (r1 - x2.astype(F32)).astype(BF16)
    return x1, x2, x3


def _rec_kernel(*refs, heads, kdim, vdim, chunk, sub, scale, has_s0, single):
    refs = list(refs)
    q_ref, k_ref, v_ref, g_ref = refs[:4]
    s0_ref = refs[4] if has_s0 else None
    o_ref, s_ref = refs[4 + has_s0:6 + has_s0]
    scratch = refs[6 + has_s0:]
    c, sb = chunk, sub
    n_sub = c // sb
    if not single:
        st_ref, scratch = scratch[0], scratch[1:]
    if n_sub > 1:
        qx_s, scratch = scratch[0], scratch[1:]
    qhat_s, kdec_s, kvar_s, dec_s, attn_s = scratch
    if n_sub == 1:
        qx_s = qhat_s
    n = pl.program_id(1)
    n_seq = q_ref.shape[0]
    hk = heads * kdim

    if not single:
        @pl.when(n == 0)
        def _init():
            if has_s0:
                for s in range(n_seq):
                    for h in range(heads):
                        st_ref[s, h] = s0_ref[s, h].T
            else:
                st_ref[...] = jnp.zeros_like(st_ref)

    causal = (lax.broadcasted_iota(jnp.int32, (c, c), 0) >= lax.broadcasted_iota(jnp.int32, (c, c), 1))
    tri = causal.astype(F32).astype(BF16)

    def prepare(s):
        q = q_ref[s]
        k = k_ref[s]
        g1, g2, g3 = _split3(g_ref[s])
        b = _dot(tri, g1) + _dot(tri, g2) + _dot(tri, g3)
        b_last = b[c - 1:c, :]
        qhat = q * jnp.exp(b) * scale
        qhat_s[s] = qhat.astype(BF16)
        kdec_s[s] = (k * jnp.exp(b_last - b)).astype(BF16)
        if single:
            p1, p2, p3 = (p.astype(F32) for p in _split3(b_last))
            r = lax.broadcasted_iota(jnp.int32, (BF16_ROWS, hk), 0)
            dec_s[s] = jnp.where(r == 0, p1, jnp.where(r == 1, p2, jnp.where(r == 2, p3, 0.0))).astype(BF16)
        else:
            dec_s[s] = jnp.exp(b_last)
        qx = [qhat[0:sb]]
        for blk in range(n_sub):
            rows = (blk + 1) * sb
            if blk == 0:
                expo = -b[:rows]
            else:
                r_blk = b[blk * sb - 1:blk * sb, :]
                expo = r_blk - b[:rows]
                rs = slice(blk * sb, rows)
                qx.append(q[rs] * jnp.exp(b[rs] - r_blk) * scale)
            kv = (k[:rows] * jnp.exp(expo)).astype(BF16)
            if rows < c:
                kv = jnp.concatenate([kv, jnp.zeros((c - rows, hk), BF16)], axis=0)
            kvar_s[s, blk] = kv
        if n_sub > 1:
            qx_s[s] = jnp.concatenate(qx, axis=0).astype(BF16)

    def scores(s):
        for h in range(heads):
            ks = slice(h * kdim, (h + 1) * kdim)
            parts = [_dot(qx_s[s, blk * sb:(blk + 1) * sb, ks], kvar_s[s, blk, :, ks], _NT)
                     for blk in range(n_sub)]
            attn = parts[0] if n_sub == 1 else jnp.concatenate(parts, axis=0)
            attn_s[s, h] = jnp.where(causal, attn, 0.0).astype(BF16)

    def update(s):
        for h in range(heads):
            ks = slice(h * kdim, (h + 1) * kdim)
            vs = slice(h * vdim, (h + 1) * vdim)
            vh = v_ref[s, :, vs]
            o_intra = _dot(attn_s[s, h], vh)
            if single:
                st = s0_ref[s, h] if has_s0 else jnp.zeros((kdim, vdim), F32)
                o_ref[s, :, vs] = o_intra + _dot(qhat_s[s, :, ks], st.astype(BF16))
                ones = jnp.ones((BF16_ROWS, LANES), BF16)
                dec = jnp.exp(_dot(dec_s[s, :, ks], ones, _TN))
                dec = jnp.concatenate([dec] * (vdim // LANES), axis=1)
                s_ref[s, h] = st * dec + _dot(kdec_s[s, :, ks], vh, _TN)
            else:
                st = st_ref[s, h]
                o_ref[s, :, vs] = o_intra + _dot(qhat_s[s, :, ks], st.astype(BF16), _NT)
                st_ref[s, h] = st * dec_s[s, :, ks] + _dot(vh, kdec_s[s, :, ks], _TN)

    for s in range(n_seq):
        prepare(s)
        scores(s)
        update(s)

    if not single:
        @pl.when(n == pl.num_programs(1) - 1)
        def _final():
            for s in range(n_seq):
                for h in range(heads):
                    s_ref[s, h] = st_ref[s, h].T


def _rec(q, k, v, g, s0, heads, chunk, sub, scale):
    n_seq, t, hk = q.shape
    hv = v.shape[-1]
    kdim, vdim = hk // heads, hv // heads
    bb = min(REC_SEQS, n_seq)
    n_sub = chunk // sub
    single = t == chunk
    grid = (n_seq // bb, t // chunk)
    tok = lambda w: pl.BlockSpec((bb, chunk, w), lambda i, j: (i, j, 0))
    state = pl.BlockSpec((bb, heads, kdim, vdim), lambda i, j: (i, 0, 0, 0))
    in_specs = [tok(hk), tok(hk), tok(hv), tok(hk)]
    args = [q, k, v, g]
    if s0 is not None:
        in_specs.append(state)
        args.append(s0)
    scratch = []
    if not single:
        scratch.append(pltpu.VMEM((bb, heads, vdim, kdim), F32))
    if n_sub > 1:
        scratch.append(pltpu.VMEM((bb, chunk, hk), BF16))
    scratch += [
        pltpu.VMEM((bb, chunk, hk), BF16),
        pltpu.VMEM((bb, chunk, hk), BF16),
        pltpu.VMEM((bb, n_sub, chunk, hk), BF16),
        pltpu.VMEM((bb, BF16_ROWS, hk), BF16) if single else pltpu.VMEM((bb, 1, hk), F32),
        pltpu.VMEM((bb, heads, chunk, chunk), BF16),
    ]
    return pl.pallas_call(
        functools.partial(_rec_kernel, heads=heads, kdim=kdim, vdim=vdim, chunk=chunk, sub=sub,
                          scale=scale, has_s0=s0 is not None, single=single),
        grid=grid,
        in_specs=in_specs,
        out_specs=[tok(hv), state],
        out_shape=[jax.ShapeDtypeStruct((n_seq, t, hv), F32),
                   jax.ShapeDtypeStruct((n_seq, heads, kdim, vdim), F32)],
        scratch_shapes=scratch,
        compiler_params=_cparams(2),
        name="rec",
    )(*args)


def _post_kernel(o_ref, og_ref, h_ref, p_ref, gn_ref, wo_ref, nw_ref, wgu_ref, wd_ref,
                 pn_ref, pwg_ref, pwp_ref, fn_ref, out_ref, xn_ref, acc_ref, *, heads, final):
    o = o_ref[...]
    vdim = o.shape[1] // heads
    gn = gn_ref[...]
    y = jnp.concatenate([_rms(o[:, h * vdim:(h + 1) * vdim], gn) for h in range(heads)], axis=1)
    y = (y * _silu(og_ref[...])).astype(BF16)
    out_ref[...] = h_ref[...] + _dot(y, wo_ref[...])
    _ffn_acc(out_ref[...], nw_ref, wgu_ref, wd_ref, xn_ref, acc_ref)
    h2 = out_ref[...] + 0.5 * acc_ref[...]
    gate = _sigmoid(_dot(_rms(h2, pn_ref[...]).astype(BF16), pwg_ref[...]))
    h3 = h2 + gate * _dot(p_ref[...].astype(BF16), pwp_ref[...])
    if final:
        h3 = _rms(h3, fn_ref[...])
    out_ref[...] = h3


def _post(o, og, h, p, mix_out, ffn, ple, li, j, heads, final, tm):
    n, hv = o.shape
    d = h.shape[1]
    gn, wo = mix_out
    nw, wgu, wd = ffn
    pn, pwg, pwp, fn = ple
    return pl.pallas_call(
        functools.partial(_post_kernel, heads=heads, final=final),
        grid=(n // tm,),
        in_specs=[_rows(tm, hv), _rows(tm, hv), _rows(tm, d),
                  pl.BlockSpec((None, tm, p.shape[2]), lambda i: (li, i, 0)),
                  _layer(gn, j), _layer(wo, j), _layer(nw, li), _layer(wgu, li), _layer(wd, li),
                  _layer(pn, li), _layer(pwg, li), _layer(pwp, li), _resident(fn.shape)],
        out_specs=_rows(tm, d),
        out_shape=jax.ShapeDtypeStruct((n, d), F32),
        scratch_shapes=[pltpu.VMEM((tm, d), BF16), pltpu.VMEM((tm, d), F32)],
        compiler_params=_cparams(1),
        name="post",
    )(o, og, h, p, gn, wo, nw, wgu, wd, pn, pwg, pwp, fn)


def _prep_weights(ffn1_norm, ffn1_w_gu, ffn1_w_down, mix_norm, gla_w_in, gla_w_gk_up, gla_b_gk, gla_gn,
                  gla_w_out, hgrn_w_in, hgrn_gn, hgrn_w_out, hgrn_lower_bounds, ffn2_norm, ffn2_w_gu,
                  ffn2_w_down, ple_norm, ple_w_gate, ple_w_proj, final_norm):
    row = lambda a: a.reshape(a.shape[0], 1, a.shape[1])
    rank = gla_w_gk_up.shape[1]
    main = gla_w_in.shape[2] - rank
    return dict(
        ffn1=(row(ffn1_norm), ffn1_w_gu.astype(BF16), ffn1_w_down.astype(BF16)),
        ffn2=(row(ffn2_norm), ffn2_w_gu.astype(BF16), ffn2_w_down.astype(BF16)),
        mix_norm=row(mix_norm),
        gla_in=(gla_w_in[:, :, :main].astype(BF16),
                jnp.pad(gla_w_in[:, :, main:], ((0, 0), (0, 0), (0, LANES - rank))).astype(BF16),
                jnp.pad(gla_w_gk_up, ((0, 0), (0, LANES - rank), (0, 0))).astype(BF16),
                row(gla_b_gk)),
        gla_out=(row(gla_gn), gla_w_out.astype(BF16)),
        hgrn_in=(hgrn_w_in.astype(BF16), hgrn_lower_bounds),
        hgrn_out=(row(hgrn_gn), hgrn_w_out.astype(BF16)),
        ple=(row(ple_norm), ple_w_gate.astype(BF16), ple_w_proj.astype(BF16), final_norm.reshape(1, -1)),
    )


def _trunk(x, p, st_gla, st_hgrn, w, gla_heads, hgrn_heads):
    bsz, t, d = x.shape
    n = bsz * t
    depth = p.shape[0]
    tm = min(ROW_TILE, n)
    chunk = REC_CHUNK if t >= REC_CHUNK else REC_SUB
    t_pad = -(-t // chunk) * chunk
    p = p.reshape(depth, n, p.shape[-1])
    h = x.reshape(n, d)
    new_gla, new_hgrn = [], []

    def seqs(a):
        a = a.reshape(bsz, t, a.shape[1])
        return a if t_pad == t else jnp.pad(a, ((0, 0), (0, t_pad - t), (0, 0)))

    for li in range(depth):
        h = _ffn(h, *w["ffn1"], li, tm)
        j = li // 2
        kind = "gla" if li % 2 == 0 else "hgrn"
        heads = gla_heads if kind == "gla" else hgrn_heads
        q, k, v, g, og = _in_proj(h, kind, (w["mix_norm"],) + w[kind + "_in"], li, j, tm)
        scale = (q.shape[1] // heads) ** -0.5 if kind == "gla" else 1.0
        s0 = st_gla if kind == "gla" else st_hgrn
        o, s = _rec(seqs(q), seqs(k), seqs(v), seqs(g), None if s0 is None else s0[j], heads, chunk,
                    min(REC_SUB, chunk), scale)
        (new_gla if kind == "gla" else new_hgrn).append(s)
        if t_pad != t:
            o = o[:, :t]
        h = _post(o.reshape(n, o.shape[-1]), og, h, p, w[kind + "_out"], w["ffn2"], w["ple"], li, j, heads,
                  li == depth - 1, tm)
    return h.reshape(bsz, t, d), jnp.stack(new_gla), jnp.stack(new_hgrn)


def kernel(x_prompt, x_sample, state_gla, state_hgrn, p_prompt, p_sample, ffn1_norm, ffn1_w_gu, ffn1_w_down, mix_norm, gla_w_in, gla_w_gk_up, gla_b_gk, gla_gn, gla_w_out, hgrn_w_in, hgrn_gn, hgrn_w_out, hgrn_lower_bounds, ffn2_norm, ffn2_w_gu, ffn2_w_down, ple_norm, ple_w_gate, ple_w_proj, final_norm):
    w = _prep_weights(ffn1_norm, ffn1_w_gu, ffn1_w_down, mix_norm, gla_w_in, gla_w_gk_up, gla_b_gk, gla_gn,
                      gla_w_out, hgrn_w_in, hgrn_gn, hgrn_w_out, hgrn_lower_bounds, ffn2_norm, ffn2_w_gu,
                      ffn2_w_down, ple_norm, ple_w_gate, ple_w_proj, final_norm)
    gla_heads = state_gla.shape[2]
    hgrn_heads = state_hgrn.shape[2]
    y_p, gla_p, hgrn_p = _trunk(x_prompt, p_prompt, None, None, w, gla_heads, hgrn_heads)
    y_s, gla_s, hgrn_s = _trunk(x_sample, p_sample, state_gla, state_hgrn, w, gla_heads, hgrn_heads)
    return (y_p, y_s, gla_p, gla_s, hgrn_p, hgrn_s)
```

```python
import functools

import jax
import jax.numpy as jnp
from jax import lax
from jax.experimental import pallas as pl
from jax.experimental.pallas import tpu as pltpu

F32 = jnp.float32
BF16 = jnp.bfloat16
RMS_EPS = 1e-6
GLA_GATE_NORM = 16.0
LANES = 128
BF16_ROWS = 16
FF_CHUNK = 256
ROW_TILE = 512
PRE_TILE = {"gla": 512, "hgrn": 512}
REC_CHUNK = 64
REC_SUB = 16
REC_SEQS = 8
VMEM_LIMIT = 62 * 1024 * 1024

_NT = (((1,), (1,)), ((), ()))
_TN = (((0,), (0,)), ((), ()))


def _cparams(n_axes):
    return pltpu.CompilerParams(dimension_semantics=("arbitrary",) * n_axes,
                                vmem_limit_bytes=VMEM_LIMIT)


def _resident(shape):
    nd = len(shape)
    return pl.BlockSpec(shape, lambda *_: (0,) * nd, pipeline_mode=pl.Buffered(1))


def _layer(arr, li):
    nd = arr.ndim - 1
    return pl.BlockSpec((None,) + arr.shape[1:], lambda *_: (li,) + (0,) * nd,
                        pipeline_mode=pl.Buffered(1))


def _rows(tm, width):
    return pl.BlockSpec((tm, width), lambda i: (i, 0))


def _dot(a, b, dims=None):
    if dims is None:
        return jnp.dot(a, b, preferred_element_type=F32)
    return lax.dot_general(a, b, dims, preferred_element_type=F32)


def _rms(x, w):
    ms = jnp.mean(x * x, axis=-1, keepdims=True)
    return x * lax.rsqrt(ms + RMS_EPS) * w


def _sigmoid(x):
    return 1.0 / (1.0 + jnp.exp(-x))


def _silu(x):
    return x * _sigmoid(x)


def _ffn_acc(x, nw_ref, wgu_ref, wd_ref, xn_ref, acc_ref):
    d_ff = wd_ref.shape[0]
    xn_ref[...] = _rms(x, nw_ref[...]).astype(BF16)
    for c in range(d_ff // FF_CHUNK):
        lo, hi = c * FF_CHUNK, (c + 1) * FF_CHUNK
        xn = xn_ref[...]
        g = _dot(xn, wgu_ref[:, lo:hi])
        u = _dot(xn, wgu_ref[:, d_ff + lo:d_ff + hi])
        part = _dot((_silu(g) * u).astype(BF16), wd_ref[lo:hi, :])
        if c == 0:
            acc_ref[...] = part
        else:
            acc_ref[...] += part


def _gla_proj(u, w_ref, wr_ref, wup_ref, bgk_ref, kd, vd):
    q = _dot(u, w_ref[:, 0:kd])
    k = _dot(u, w_ref[:, kd:2 * kd])
    v = _dot(u, w_ref[:, 2 * kd:2 * kd + vd])
    og = _dot(u, w_ref[:, 2 * kd + vd:2 * kd + 2 * vd])
    r = _dot(u, wr_ref[...]).astype(BF16)
    x = _dot(r, wup_ref[...]) + bgk_ref[...]
    g = (jnp.minimum(x, 0.0) - jnp.log1p(jnp.exp(-jnp.abs(x)))) * (1.0 / GLA_GATE_NORM)
    return q, k, v, g, og


def _hgrn_proj(u, w_ref, lbw_ref, fd, layer):
    lbw = lbw_ref[...]
    rows = [lbw[i:i + 1, :] for i in range(lbw.shape[0])]
    m = functools.reduce(jnp.maximum, rows)
    es = [jnp.exp(r - m) for r in rows]
    tot = functools.reduce(jnp.add, es)
    sm = [e / tot for e in es]
    lb = functools.reduce(jnp.add, sm[:layer + 1]) - sm[0]

    q = _silu(_dot(u, w_ref[:, 0:fd]))
    f = _dot(u, w_ref[:, fd:2 * fd])
    e = jnp.exp(-jnp.abs(f))
    inv = 1.0 / (1.0 + e)
    pos = f >= 0.0
    sig_f = jnp.where(pos, inv, e * inv)
    sig_nf = jnp.where(pos, e * inv, inv)
    g = jnp.log(lb + (1.0 - lb) * sig_f)
    k = (1.0 - lb) * sig_nf
    v = _dot(u, w_ref[:, 2 * fd:3 * fd])
    og = _dot(u, w_ref[:, 3 * fd:4 * fd])
    return q, k, v, g, og


def _pre_kernel(*refs, kind, hk, hv, layer):
    x_ref, fnw_ref, wgu_ref, wd_ref, nw_ref, w_ref = refs[:6]
    if kind == "gla":
        wr_ref, wup_ref, bgk_ref = refs[6:9]
        rest = refs[9:]
    else:
        lbw_ref = refs[6]
        rest = refs[7:]
    h_ref, q_ref, k_ref, v_ref, g_ref, og_ref, xn_ref, acc_ref = rest
    x = x_ref[...]
    _ffn_acc(x, fnw_ref, wgu_ref, wd_ref, xn_ref, acc_ref)
    h = x + 0.5 * acc_ref[...]
    h_ref[...] = h
    u = _rms(h, nw_ref[...]).astype(BF16)
    if kind == "gla":
        q, k, v, g, og = _gla_proj(u, w_ref, wr_ref, wup_ref, bgk_ref, hk, hv)
    else:
        q, k, v, g, og = _hgrn_proj(u, w_ref, lbw_ref, hk, layer)
    q_ref[...] = q
    k_ref[...] = k
    v_ref[...] = v.astype(BF16)
    g_ref[...] = g
    og_ref[...] = og


def _pre(x, kind, ffn, params, li, j, tm):
    n, d = x.shape
    fnw, wgu, wd = ffn
    if kind == "gla":
        nw, w, wr, wup, bgk = params
        hk, hv = wup.shape[2], (w.shape[2] - 2 * wup.shape[2]) // 2
        in_specs = [_layer(nw, li), _layer(w, j), _layer(wr, j), _layer(wup, j), _layer(bgk, j)]
    else:
        nw, w, lbw = params
        hk = hv = w.shape[2] // 4
        in_specs = [_layer(nw, li), _layer(w, j), _resident(lbw.shape)]
    outs = [((n, d), F32), ((n, hk), F32), ((n, hk), F32), ((n, hv), BF16), ((n, hk), F32), ((n, hv), F32)]
    return pl.pallas_call(
        functools.partial(_pre_kernel, kind=kind, hk=hk, hv=hv, layer=li),
        grid=(n // tm,),
        in_specs=[_rows(tm, d), _layer(fnw, li), _layer(wgu, li), _layer(wd, li)] + in_specs,
        out_specs=[_rows(tm, s[0][1]) for s in outs],
        out_shape=[jax.ShapeDtypeStruct(*s) for s in outs],
        scratch_shapes=[pltpu.VMEM((tm, d), BF16), pltpu.VMEM((tm, d), F32)],
        compiler_params=_cparams(1),
        name="pre_" + kind,
    )(x, fnw, wgu, wd, *params)


def _split3(x):
    x1 = x.astype(BF16)
    r1 = x - x1.astype(F32)
    x2 = r1.astype(BF16)
    x3 = (r1 - x2.astype(F32)).astype(BF16)
    return x1, x2, x3


def _rec_kernel(*refs, heads, kdim, vdim, chunk, sub, scale, has_s0, single):
    refs = list(refs)
    q_ref, k_ref, v_ref, g_ref = refs[:4]
    s0_ref = refs[4] if has_s0 else None
    o_ref, s_ref = refs[4 + has_s0:6 + has_s0]
    scratch = refs[6 + has_s0:]
    c, sb = chunk, sub
    n_sub = c // sb
    if not single:
        st_ref, scratch = scratch[0], scratch[1:]
    if n_sub > 1:
        qx_s, scratch = scratch[0], scratch[1:]
    qhat_s, kdec_s, kvar_s, dec_s, attn_s = scratch
    if n_sub == 1:
        qx_s = qhat_s
    n = pl.program_id(1)
    n_seq = q_ref.shape[0]
    hk = heads * kdim

    if not single:
        @pl.when(n == 0)
        def _init():
            if has_s0:
                for s in range(n_seq):
                    for h in range(heads):
                        st_ref[s, h] = s0_ref[s, h].T
            else:
                st_ref[...] = jnp.zeros_like(st_ref)

    causal = (lax.broadcasted_iota(jnp.int32, (c, c), 0) >= lax.broadcasted_iota(jnp.int32, (c, c), 1))
    tri = causal.astype(F32).astype(BF16)

    def prepare(s):
        q = q_ref[s] * scale
        k = k_ref[s]
        g = g_ref[s]
        g1 = g.astype(BF16)
        g2 = (g - g1.astype(F32)).astype(BF16)
        b = _dot(tri, g1) + _dot(tri, g2)
        blk_rows = [slice(i * sb, (i + 1) * sb) for i in range(n_sub)]
        eb = [jnp.exp(b[r] if i == 0 else b[r] - b[i * sb - 1:i * sb, :]) for i, r in enumerate(blk_rows)]
        blk_dec = [e[sb - 1:sb, :] for e in eb]

        def span(lo, hi):
            out = None
            for i in range(lo, hi):
                out = blk_dec[i] if out is None else out * blk_dec[i]
            return out

        mul = lambda x, d: x if d is None else x * d
        qx = [q[r] * eb[i] for i, r in enumerate(blk_rows)]
        kdiag = [k[r] * (1.0 / eb[i]) for i, r in enumerate(blk_rows)]
        kend = [kdiag[i] * blk_dec[i] for i in range(n_sub)]
        cat = lambda parts: parts[0] if len(parts) == 1 else jnp.concatenate(parts, axis=0)
        qhat_s[s] = cat([mul(qx[i], span(0, i)) for i in range(n_sub)]).astype(BF16)
        kdec_s[s] = cat([mul(kend[i], span(i + 1, n_sub)) for i in range(n_sub)]).astype(BF16)
        if n_sub > 1:
            qx_s[s] = cat(qx).astype(BF16)
        for i in range(n_sub):
            parts = [mul(kend[j], span(j + 1, i)) for j in range(i)] + [kdiag[i]]
            kv = cat(parts).astype(BF16)
            if i + 1 < n_sub:
                kv = jnp.concatenate([kv, jnp.zeros((c - (i + 1) * sb, hk), BF16)], axis=0)
            kvar_s[s, i] = kv
        if single:
            p1, p2, p3 = (p.astype(F32) for p in _split3(b[c - 1:c, :]))
            r = lax.broadcasted_iota(jnp.int32, (BF16_ROWS, hk), 0)
            dec_s[s] = jnp.where(r == 0, p1, jnp.where(r == 1, p2, jnp.where(r == 2, p3, 0.0))).astype(BF16)
        else:
            dec_s[s] = span(0, n_sub)

    def scores(s):
        for h in range(heads):
            ks = slice(h * kdim, (h + 1) * kdim)
            parts = [_dot(qx_s[s, blk * sb:(blk + 1) * sb, ks], kvar_s[s, blk, :, ks], _NT)
                     for blk in range(n_sub)]
            attn = parts[0] if n_sub == 1 else jnp.concatenate(parts, axis=0)
            attn_s[s, h] = jnp.where(causal, attn, 0.0).astype(BF16)

    def update(s):
        for h in range(heads):
            ks = slice(h * kdim, (h + 1) * kdim)
            vs = slice(h * vdim, (h + 1) * vdim)
            vh = v_ref[s, :, vs]
            o_intra = _dot(attn_s[s, h], vh)
            if single:
                st = s0_ref[s, h] if has_s0 else jnp.zeros((kdim, vdim), F32)
                o_ref[s, :, vs] = o_intra + _dot(qhat_s[s, :, ks], st.astype(BF16))
                ones = jnp.ones((BF16_ROWS, LANES), BF16)
                dec = jnp.exp(_dot(dec_s[s, :, ks], ones, _TN))
                dec = jnp.concatenate([dec] * (vdim // LANES), axis=1)
                s_ref[s, h] = st * dec + _dot(kdec_s[s, :, ks], vh, _TN)
            else:
                st = st_ref[s, h]
                o_ref[s, :, vs] = o_intra + _dot(qhat_s[s, :, ks], st.astype(BF16), _NT)
                st_ref[s, h] = st * dec_s[s, :, ks] + _dot(vh, kdec_s[s, :, ks], _TN)

    for s in range(n_seq):
        prepare(s)
        scores(s)
        update(s)

    if not single:
        @pl.when(n == pl.num_programs(1) - 1)
        def _final():
            for s in range(n_seq):
                for h in range(heads):
                    s_ref[s, h] = st_ref[s, h].T


def _rec(q, k, v, g, s0, heads, chunk, sub, scale):
    n_seq, t, hk = q.shape
    hv = v.shape[-1]
    kdim, vdim = hk // heads, hv // heads
    bb = min(REC_SEQS, n_seq)
    n_sub = chunk // sub
    single = t == chunk
    grid = (n_seq // bb, t // chunk)
    tok = lambda w: pl.BlockSpec((bb, chunk, w), lambda i, j: (i, j, 0))
    state = pl.BlockSpec((bb, heads, kdim, vdim), lambda i, j: (i, 0, 0, 0))
    in_specs = [tok(hk), tok(hk), tok(hv), tok(hk)]
    args = [q, k, v, g]
    if s0 is not None:
        in_specs.append(state)
        args.append(s0)
    scratch = []
    if not single:
        scratch.append(pltpu.VMEM((bb, heads, vdim, kdim), F32))
    if n_sub > 1:
        scratch.append(pltpu.VMEM((bb, chunk, hk), BF16))
    scratch += [
        pltpu.VMEM((bb, chunk, hk), BF16),
        pltpu.VMEM((bb, chunk, hk), BF16),
        pltpu.VMEM((bb, n_sub, chunk, hk), BF16),
        pltpu.VMEM((bb, BF16_ROWS, hk), BF16) if single else pltpu.VMEM((bb, 1, hk), F32),
        pltpu.VMEM((bb, heads, chunk, chunk), BF16),
    ]
    return pl.pallas_call(
        functools.partial(_rec_kernel, heads=heads, kdim=kdim, vdim=vdim, chunk=chunk, sub=sub,
                          scale=scale, has_s0=s0 is not None, single=single),
        grid=grid,
        in_specs=in_specs,
        out_specs=[tok(hv), state],
        out_shape=[jax.ShapeDtypeStruct((n_seq, t, hv), F32),
                   jax.ShapeDtypeStruct((n_seq, heads, kdim, vdim), F32)],
        scratch_shapes=scratch,
        compiler_params=_cparams(2),
        name="rec",
    )(*args)


def _post_kernel(o_ref, og_ref, h_ref, p_ref, gn_ref, wo_ref, nw_ref, wgu_ref, wd_ref,
                 pn_ref, pwg_ref, pwp_ref, fn_ref, out_ref, xn_ref, acc_ref, *, heads, final):
    o = o_ref[...]
    vdim = o.shape[1] // heads
    gn = gn_ref[...]
    y = jnp.concatenate([_rms(o[:, h * vdim:(h + 1) * vdim], gn) for h in range(heads)], axis=1)
    y = (y * _silu(og_ref[...])).astype(BF16)
    out_ref[...] = h_ref[...] + _dot(y, wo_ref[...])
    _ffn_acc(out_ref[...], nw_ref, wgu_ref, wd_ref, xn_ref, acc_ref)
    h2 = out_ref[...] + 0.5 * acc_ref[...]
    gate = _sigmoid(_dot(_rms(h2, pn_ref[...]).astype(BF16), pwg_ref[...]))
    h3 = h2 + gate * _dot(p_ref[...].astype(BF16), pwp_ref[...])
    if final:
        h3 = _rms(h3, fn_ref[...])
    out_ref[...] = h3


def _post(o, og, h, p, mix_out, ffn, ple, li, j, heads, final, tm):
    n, hv = o.shape
    d = h.shape[1]
    gn, wo = mix_out
    nw, wgu, wd = ffn
    pn, pwg, pwp, fn = ple
    return pl.pallas_call(
        functools.partial(_post_kernel, heads=heads, final=final),
        grid=(n // tm,),
        in_specs=[_rows(tm, hv), _rows(tm, hv), _rows(tm, d),
                  pl.BlockSpec((None, tm, p.shape[2]), lambda i: (li, i, 0)),
                  _layer(gn, j), _layer(wo, j), _layer(nw, li), _layer(wgu, li), _layer(wd, li),
                  _layer(pn, li), _layer(pwg, li), _layer(pwp, li), _resident(fn.shape)],
        out_specs=_rows(tm, d),
        out_shape=jax.ShapeDtypeStruct((n, d), F32),
        scratch_shapes=[pltpu.VMEM((tm, d), BF16), pltpu.VMEM((tm, d), F32)],
        compiler_params=_cparams(1),
        name="post",
    )(o, og, h, p, gn, wo, nw, wgu, wd, pn, pwg, pwp, fn)


def _prep_weights(ffn1_norm, ffn1_w_gu, ffn1_w_down, mix_norm, gla_w_in, gla_w_gk_up, gla_b_gk, gla_gn,
                  gla_w_out, hgrn_w_in, hgrn_gn, hgrn_w_out, hgrn_lower_bounds, ffn2_norm, ffn2_w_gu,
                  ffn2_w_down, ple_norm, ple_w_gate, ple_w_proj, final_norm):
    row = lambda a: a.reshape(a.shape[0], 1, a.shape[1])
    rank = gla_w_gk_up.shape[1]
    main = gla_w_in.shape[2] - rank
    return dict(
        ffn1=(row(ffn1_norm), ffn1_w_gu.astype(BF16), ffn1_w_down.astype(BF16)),
        ffn2=(row(ffn2_norm), ffn2_w_gu.astype(BF16), ffn2_w_down.astype(BF16)),
        mix_norm=row(mix_norm),
        gla_in=(gla_w_in[:, :, :main].astype(BF16),
                jnp.pad(gla_w_in[:, :, main:], ((0, 0), (0, 0), (0, LANES - rank))).astype(BF16),
                jnp.pad(gla_w_gk_up, ((0, 0), (0, LANES - rank), (0, 0))).astype(BF16),
                row(gla_b_gk)),
        gla_out=(row(gla_gn), gla_w_out.astype(BF16)),
        hgrn_in=(hgrn_w_in.astype(BF16), hgrn_lower_bounds),
        hgrn_out=(row(hgrn_gn), hgrn_w_out.astype(BF16)),
        ple=(row(ple_norm), ple_w_gate.astype(BF16), ple_w_proj.astype(BF16), final_norm.reshape(1, -1)),
    )


def _trunk(x, p, st_gla, st_hgrn, w, gla_heads, hgrn_heads):
    bsz, t, d = x.shape
    n = bsz * t
    depth = p.shape[0]
    tm = min(ROW_TILE, n)
    chunk = REC_CHUNK if t >= REC_CHUNK else REC_SUB
    t_pad = -(-t // chunk) * chunk
    p = p.reshape(depth, n, p.shape[-1])
    h = x.reshape(n, d)
    new_gla, new_hgrn = [], []

    def seqs(a):
        a = a.reshape(bsz, t, a.shape[1])
        return a if t_pad == t else jnp.pad(a, ((0, 0), (0, t_pad - t), (0, 0)))

    for li in range(depth):
        j = li // 2
        kind = "gla" if li % 2 == 0 else "hgrn"
        heads = gla_heads if kind == "gla" else hgrn_heads
        h, q, k, v, g, og = _pre(h, kind, w["ffn1"], (w["mix_norm"],) + w[kind + "_in"], li, j,
                                 min(PRE_TILE[kind], n))
        scale = (q.shape[1] // heads) ** -0.5 if kind == "gla" else 1.0
        s0 = st_gla if kind == "gla" else st_hgrn
        o, s = _rec(seqs(q), seqs(k), seqs(v), seqs(g), None if s0 is None else s0[j], heads, chunk,
                    min(REC_SUB, chunk), scale)
        (new_gla if kind == "gla" else new_hgrn).append(s)
        if t_pad != t:
            o = o[:, :t]
        h = _post(o.reshape(n, o.shape[-1]), og, h, p, w[kind + "_out"], w["ffn2"], w["ple"], li, j, heads,
                  li == depth - 1, tm)
    return h.reshape(bsz, t, d), jnp.stack(new_gla), jnp.stack(new_hgrn)


def kernel(x_prompt, x_sample, state_gla, state_hgrn, p_prompt, p_sample, ffn1_norm, ffn1_w_gu, ffn1_w_down, mix_norm, gla_w_in, gla_w_gk_up, gla_b_gk, gla_gn, gla_w_out, hgrn_w_in, hgrn_gn, hgrn_w_out, hgrn_lower_bounds, ffn2_norm, ffn2_w_gu, ffn2_w_down, ple_norm, ple_w_gate, ple_w_proj, final_norm):
    w = _prep_weights(ffn1_norm, ffn1_w_gu, ffn1_w_down, mix_norm, gla_w_in, gla_w_gk_up, gla_b_gk, gla_gn,
                      gla_w_out, hgrn_w_in, hgrn_gn, hgrn_w_out, hgrn_lower_bounds, ffn2_norm, ffn2_w_gu,
                      ffn2_w_down, ple_norm, ple_w_gate, ple_w_proj, final_norm)
    gla_heads = state_gla.shape[2]
    hgrn_heads = state_hgrn.shape[2]
    y_p, gla_p, hgrn_p = _trunk(x_prompt, p_prompt, None, None, w, gla_heads, hgrn_heads)
    y_s, gla_s, hgrn_s = _trunk(x_sample, p_sample, state_gla, state_hgrn, w, gla_heads, hgrn_heads)
    return (y_p, y_s, gla_p, gla_s, hgrn_p, hgrn_s)
```

```python
import functools

import jax
import jax.numpy as jnp
from jax import lax
from jax.experimental import pallas as pl
from jax.experimental.pallas import tpu as pltpu

F32 = jnp.float32
BF16 = jnp.bfloat16
RMS_EPS = 1e-6
GLA_GATE_NORM = 16.0
LANES = 128
BF16_ROWS = 16
FF_CHUNK = 256
ROW_TILE = 512
PRE_TILE = {"gla": 512, "hgrn": 512}
REC_CHUNK = 64
REC_SUB = 16
REC_SEQS = 8
VMEM_LIMIT = 62 * 1024 * 1024

_NT = (((1,), (1,)), ((), ()))
_TN = (((0,), (0,)), ((), ()))


def _cparams(n_axes):
    return pltpu.CompilerParams(dimension_semantics=("arbitrary",) * n_axes,
                                vmem_limit_bytes=VMEM_LIMIT)


def _resident(shape):
    nd = len(shape)
    return pl.BlockSpec(shape, lambda *_: (0,) * nd, pipeline_mode=pl.Buffered(1))


def _layer(arr, li):
    nd = arr.ndim - 1
    return pl.BlockSpec((None,) + arr.shape[1:], lambda *_: (li,) + (0,) * nd,
                        pipeline_mode=pl.Buffered(1))


def _rows(tm, width):
    return pl.BlockSpec((tm, width), lambda i: (i, 0))


def _dot(a, b, dims=None):
    if dims is None:
        return jnp.dot(a, b, preferred_element_type=F32)
    return lax.dot_general(a, b, dims, preferred_element_type=F32)


def _rms(x, w):
    ms = jnp.mean(x * x, axis=-1, keepdims=True)
    return x * lax.rsqrt(ms + RMS_EPS) * w


def _sigmoid(x):
    return 1.0 / (1.0 + jnp.exp(-x))


def _silu(x):
    return x * _sigmoid(x)


def _ffn_acc(x, nw_ref, wgu_ref, wd_ref, xn_ref, acc_ref):
    d_ff = wd_ref.shape[0]
    xn_ref[...] = _rms(x, nw_ref[...]).astype(BF16)
    for c in range(d_ff // FF_CHUNK):
        lo, hi = c * FF_CHUNK, (c + 1) * FF_CHUNK
        xn = xn_ref[...]
        g = _dot(xn, wgu_ref[:, lo:hi])
        u = _dot(xn, wgu_ref[:, d_ff + lo:d_ff + hi])
        part = _dot((_silu(g) * u).astype(BF16), wd_ref[lo:hi, :])
        if c == 0:
            acc_ref[...] = part
        else:
            acc_ref[...] += part


def _gla_proj(u, w_ref, wr_ref, wup_ref, bgk_ref, kd, vd):
    q = _dot(u, w_ref[:, 0:kd])
    k = _dot(u, w_ref[:, kd:2 * kd])
    v = _dot(u, w_ref[:, 2 * kd:2 * kd + vd])
    og = _dot(u, w_ref[:, 2 * kd + vd:2 * kd + 2 * vd])
    r = _dot(u, wr_ref[...]).astype(BF16)
    x = _dot(r, wup_ref[...]) + bgk_ref[...]
    g = (jnp.minimum(x, 0.0) - jnp.log1p(jnp.exp(-jnp.abs(x)))) * (1.0 / GLA_GATE_NORM)
    return q, k, v, g, og


def _hgrn_proj(u, w_ref, lbw_ref, fd, layer):
    lbw = lbw_ref[...]
    rows = [lbw[i:i + 1, :] for i in range(lbw.shape[0])]
    m = functools.reduce(jnp.maximum, rows)
    es = [jnp.exp(r - m) for r in rows]
    tot = functools.reduce(jnp.add, es)
    sm = [e / tot for e in es]
    lb = functools.reduce(jnp.add, sm[:layer + 1]) - sm[0]

    q = _silu(_dot(u, w_ref[:, 0:fd]))
    f = _dot(u, w_ref[:, fd:2 * fd])
    e = jnp.exp(-jnp.abs(f))
    inv = 1.0 / (1.0 + e)
    pos = f >= 0.0
    sig_f = jnp.where(pos, inv, e * inv)
    sig_nf = jnp.where(pos, e * inv, inv)
    g = jnp.log(lb + (1.0 - lb) * sig_f)
    k = (1.0 - lb) * sig_nf
    v = _dot(u, w_ref[:, 2 * fd:3 * fd])
    og = _dot(u, w_ref[:, 3 * fd:4 * fd])
    return q, k, v, g, og


def _pre_kernel(*refs, kind, hk, hv, layer):
    x_ref, fnw_ref, wgu_ref, wd_ref, nw_ref, w_ref = refs[:6]
    if kind == "gla":
        wr_ref, wup_ref, bgk_ref = refs[6:9]
        rest = refs[9:]
    else:
        lbw_ref = refs[6]
        rest = refs[7:]
    h_ref, q_ref, k_ref, v_ref, g_ref, og_ref, xn_ref, acc_ref = rest
    x = x_ref[...]
    _ffn_acc(x, fnw_ref, wgu_ref, wd_ref, xn_ref, acc_ref)
    h = x + 0.5 * acc_ref[...]
    h_ref[...] = h
    u = _rms(h, nw_ref[...]).astype(BF16)
    if kind == "gla":
        q, k, v, g, og = _gla_proj(u, w_ref, wr_ref, wup_ref, bgk_ref, hk, hv)
    else:
        q, k, v, g, og = _hgrn_proj(u, w_ref, lbw_ref, hk, layer)
    q_ref[...] = q
    k_ref[...] = k
    v_ref[...] = v.astype(BF16)
    g_ref[...] = g
    og_ref[...] = og


def _pre(x, kind, ffn, params, li, j, tm):
    n, d = x.shape
    fnw, wgu, wd = ffn
    if kind == "gla":
        nw, w, wr, wup, bgk = params
        hk, hv = wup.shape[2], (w.shape[2] - 2 * wup.shape[2]) // 2
        in_specs = [_layer(nw, li), _layer(w, j), _layer(wr, j), _layer(wup, j), _layer(bgk, j)]
    else:
        nw, w, lbw = params
        hk = hv = w.shape[2] // 4
        in_specs = [_layer(nw, li), _layer(w, j), _resident(lbw.shape)]
    outs = [((n, d), F32), ((n, hk), F32), ((n, hk), F32), ((n, hv), BF16), ((n, hk), F32), ((n, hv), F32)]
    return pl.pallas_call(
        functools.partial(_pre_kernel, kind=kind, hk=hk, hv=hv, layer=li),
        grid=(n // tm,),
        in_specs=[_rows(tm, d), _layer(fnw, li), _layer(wgu, li), _layer(wd, li)] + in_specs,
        out_specs=[_rows(tm, s[0][1]) for s in outs],
        out_shape=[jax.ShapeDtypeStruct(*s) for s in outs],
        scratch_shapes=[pltpu.VMEM((tm, d), BF16), pltpu.VMEM((tm, d), F32)],
        compiler_params=_cparams(1),
        name="pre_" + kind,
    )(x, fnw, wgu, wd, *params)


def _split3(x):
    x1 = x.astype(BF16)
    r1 = x - x1.astype(F32)
    x2 = r1.astype(BF16)
    x3 = (r1 - x2.astype(F32)).astype(BF16)
    return x1, x2, x3


def _rec_kernel(*refs, heads, kdim, vdim, chunk, sub, scale, single):
    refs = list(refs)
    q_ref, k_ref, v_ref, g_ref = refs[:4]
    if single:
        s0_ref, o_ref, s_ref = refs[4:7]
        scratch = refs[7:]
    else:
        o_ref, s_ref, st_ref = refs[4:7]
        scratch = refs[7:]
    c, sb = chunk, sub
    n_sub = c // sb
    if n_sub > 1:
        qx_s, scratch = scratch[0], scratch[1:]
    qhat_s, kdec_s, kvar_s, dec_s, attn_s = scratch
    if n_sub == 1:
        qx_s = qhat_s
    n = pl.program_id(1)
    n_seq = q_ref.shape[0]
    hk = heads * kdim

    if not single:
        @pl.when(n == 0)
        def _init():
            st_ref[...] = jnp.zeros_like(st_ref)

    causal = (lax.broadcasted_iota(jnp.int32, (c, c), 0) >= lax.broadcasted_iota(jnp.int32, (c, c), 1))
    tri = causal.astype(F32).astype(BF16)
    tri3 = jnp.concatenate([tri, tri, tri], axis=1)

    def cumsum(s):
        return _dot(tri3, jnp.concatenate(_split3(g_ref[s]), axis=0))

    def prepare(s, b):
        q = q_ref[s] * scale
        k = k_ref[s]
        blk_rows = [slice(i * sb, (i + 1) * sb) for i in range(n_sub)]
        eb = [jnp.exp(b[r] if i == 0 else b[r] - b[i * sb - 1:i * sb, :]) for i, r in enumerate(blk_rows)]
        blk_dec = [e[sb - 1:sb, :] for e in eb]

        def span(lo, hi):
            out = None
            for i in range(lo, hi):
                out = blk_dec[i] if out is None else out * blk_dec[i]
            return out

        mul = lambda x, d: x if d is None else x * d
        qx = [q[r] * eb[i] for i, r in enumerate(blk_rows)]
        kdiag = [k[r] * (1.0 / eb[i]) for i, r in enumerate(blk_rows)]
        kend = [kdiag[i] * blk_dec[i] for i in range(n_sub)]
        cat = lambda parts: parts[0] if len(parts) == 1 else jnp.concatenate(parts, axis=0)
        qhat_s[s] = cat([mul(qx[i], span(0, i)) for i in range(n_sub)]).astype(BF16)
        kdec_s[s] = cat([mul(kend[i], span(i + 1, n_sub)) for i in range(n_sub)]).astype(BF16)
        if n_sub > 1:
            qx_s[s] = cat(qx).astype(BF16)
        for i in range(n_sub):
            parts = [mul(kend[j], span(j + 1, i)) for j in range(i)] + [kdiag[i]]
            kv = cat(parts).astype(BF16)
            if i + 1 < n_sub:
                kv = jnp.concatenate([kv, jnp.zeros((c - (i + 1) * sb, hk), BF16)], axis=0)
            kvar_s[s, i] = kv
        if single:
            p1, p2, p3 = (p.astype(F32) for p in _split3(b[c - 1:c, :]))
            r = lax.broadcasted_iota(jnp.int32, (BF16_ROWS, hk), 0)
            dec_s[s] = jnp.where(r == 0, p1, jnp.where(r == 1, p2, jnp.where(r == 2, p3, 0.0))).astype(BF16)
        else:
            dec_s[s] = span(0, n_sub)

    def scores(s, h):
        ks = slice(h * kdim, (h + 1) * kdim)
        parts = [_dot(qx_s[s, blk * sb:(blk + 1) * sb, ks], kvar_s[s, blk, :, ks], _NT)
                 for blk in range(n_sub)]
        attn = parts[0] if n_sub == 1 else jnp.concatenate(parts, axis=0)
        attn_s[s, h] = jnp.where(causal, attn, 0.0).astype(BF16)

    def update(s, h):
        ks = slice(h * kdim, (h + 1) * kdim)
        vs = slice(h * vdim, (h + 1) * vdim)
        vh = v_ref[s, :, vs]
        o_intra = _dot(attn_s[s, h], vh)
        if single:
            st = s0_ref[s, h]
            o_ref[s, :, vs] = o_intra + _dot(qhat_s[s, :, ks], st.astype(BF16))
            ones = jnp.ones((BF16_ROWS, LANES), BF16)
            dec = jnp.exp(_dot(dec_s[s, :, ks], ones, _TN))
            dec = jnp.concatenate([dec] * (vdim // LANES), axis=1)
            s_ref[s, h] = st * dec + _dot(kdec_s[s, :, ks], vh, _TN)
        else:
            st = st_ref[s, h]
            o_ref[s, :, vs] = o_intra + _dot(qhat_s[s, :, ks], st.astype(BF16), _NT)
            st_ref[s, h] = st * dec_s[s, :, ks] + _dot(vh, kdec_s[s, :, ks], _TN)

    pairs = [(s, h) for h in range(heads) for s in range(n_seq)]
    bs = [cumsum(s) for s in range(n_seq)]
    for s in range(n_seq):
        prepare(s, bs[s])
    for s, h in pairs:
        scores(s, h)
    for s, h in pairs:
        update(s, h)

    if not single:
        @pl.when(n == pl.num_programs(1) - 1)
        def _final():
            for s, h in pairs:
                s_ref[s, h] = st_ref[s, h].T


def _rec(q, k, v, g, s0, heads, chunk, sub, scale):
    n_seq, t, hk = q.shape
    hv = v.shape[-1]
    kdim, vdim = hk // heads, hv // heads
    bb = min(REC_SEQS, n_seq)
    n_sub = chunk // sub
    single = s0 is not None
    assert t == chunk if single else t % chunk == 0
    grid = (n_seq // bb, t // chunk)
    tok = lambda w: pl.BlockSpec((bb, chunk, w), lambda i, j: (i, j, 0))
    state = pl.BlockSpec((bb, heads, kdim, vdim), lambda i, j: (i, 0, 0, 0))
    in_specs = [tok(hk), tok(hk), tok(hv), tok(hk)]
    args = [q, k, v, g]
    if single:
        in_specs.append(state)
        args.append(s0)
    scratch = []
    if not single:
        scratch.append(pltpu.VMEM((bb, heads, vdim, kdim), F32))
    if n_sub > 1:
        scratch.append(pltpu.VMEM((bb, chunk, hk), BF16))
    scratch += [
        pltpu.VMEM((bb, chunk, hk), BF16),
        pltpu.VMEM((bb, chunk, hk), BF16),
        pltpu.VMEM((bb, n_sub, chunk, hk), BF16),
        pltpu.VMEM((bb, BF16_ROWS, hk), BF16) if single else pltpu.VMEM((bb, 1, hk), F32),
        pltpu.VMEM((bb, heads, chunk, chunk), BF16),
    ]
    return pl.pallas_call(
        functools.partial(_rec_kernel, heads=heads, kdim=kdim, vdim=vdim, chunk=chunk, sub=sub,
                          scale=scale, single=single),
        grid=grid,
        in_specs=in_specs,
        out_specs=[tok(hv), state],
        out_shape=[jax.ShapeDtypeStruct((n_seq, t, hv), F32),
                   jax.ShapeDtypeStruct((n_seq, heads, kdim, vdim), F32)],
        scratch_shapes=scratch,
        compiler_params=_cparams(2),
        name="rec",
    )(*args)


def _post_kernel(o_ref, og_ref, h_ref, p_ref, gn_ref, wo_ref, nw_ref, wgu_ref, wd_ref,
                 pn_ref, pwg_ref, pwp_ref, fn_ref, out_ref, xn_ref, acc_ref, *, heads, final):
    o = o_ref[...]
    vdim = o.shape[1] // heads
    gn = gn_ref[...]
    y = jnp.concatenate([_rms(o[:, h * vdim:(h + 1) * vdim], gn) for h in range(heads)], axis=1)
    y = (y * _silu(og_ref[...])).astype(BF16)
    out_ref[...] = h_ref[...] + _dot(y, wo_ref[...])
    _ffn_acc(out_ref[...], nw_ref, wgu_ref, wd_ref, xn_ref, acc_ref)
    h2 = out_ref[...] + 0.5 * acc_ref[...]
    gate = _sigmoid(_dot(_rms(h2, pn_ref[...]).astype(BF16), pwg_ref[...]))
    h3 = h2 + gate * _dot(p_ref[...].astype(BF16), pwp_ref[...])
    if final:
        h3 = _rms(h3, fn_ref[...])
    out_ref[...] = h3


def _post(o, og, h, p, mix_out, ffn, ple, li, j, heads, final, tm):
    n, hv = o.shape
    d = h.shape[1]
    gn, wo = mix_out
    nw, wgu, wd = ffn
    pn, pwg, pwp, fn = ple
    return pl.pallas_call(
        functools.partial(_post_kernel, heads=heads, final=final),
        grid=(n // tm,),
        in_specs=[_rows(tm, hv), _rows(tm, hv), _rows(tm, d),
                  pl.BlockSpec((None, tm, p.shape[2]), lambda i: (li, i, 0)),
                  _layer(gn, j), _layer(wo, j), _layer(nw, li), _layer(wgu, li), _layer(wd, li),
                  _layer(pn, li), _layer(pwg, li), _layer(pwp, li), _resident(fn.shape)],
        out_specs=_rows(tm, d),
        out_shape=jax.ShapeDtypeStruct((n, d), F32),
        scratch_shapes=[pltpu.VMEM((tm, d), BF16), pltpu.VMEM((tm, d), F32)],
        compiler_params=_cparams(1),
        name="post",
    )(o, og, h, p, gn, wo, nw, wgu, wd, pn, pwg, pwp, fn)


def _prep_weights(ffn1_norm, ffn1_w_gu, ffn1_w_down, mix_norm, gla_w_in, gla_w_gk_up, gla_b_gk, gla_gn,
                  gla_w_out, hgrn_w_in, hgrn_gn, hgrn_w_out, hgrn_lower_bounds, ffn2_norm, ffn2_w_gu,
                  ffn2_w_down, ple_norm, ple_w_gate, ple_w_proj, final_norm):
    row = lambda a: a.reshape(a.shape[0], 1, a.shape[1])
    rank = gla_w_gk_up.shape[1]
    main = gla_w_in.shape[2] - rank
    return dict(
        ffn1=(row(ffn1_norm), ffn1_w_gu.astype(BF16), ffn1_w_down.astype(BF16)),
        ffn2=(row(ffn2_norm), ffn2_w_gu.astype(BF16), ffn2_w_down.astype(BF16)),
        mix_norm=row(mix_norm),
        gla_in=(gla_w_in[:, :, :main].astype(BF16),
                jnp.pad(gla_w_in[:, :, main:], ((0, 0), (0, 0), (0, LANES - rank))).astype(BF16),
                jnp.pad(gla_w_gk_up, ((0, 0), (0, LANES - rank), (0, 0))).astype(BF16),
                row(gla_b_gk)),
        gla_out=(row(gla_gn), gla_w_out.astype(BF16)),
        hgrn_in=(hgrn_w_in.astype(BF16), hgrn_lower_bounds),
        hgrn_out=(row(hgrn_gn), hgrn_w_out.astype(BF16)),
        ple=(row(ple_norm), ple_w_gate.astype(BF16), ple_w_proj.astype(BF16), final_norm.reshape(1, -1)),
    )


def _trunk(x, p, st_gla, st_hgrn, w, gla_heads, hgrn_heads):
    bsz, t, d = x.shape
    n = bsz * t
    depth = p.shape[0]
    tm = min(ROW_TILE, n)
    chunk = REC_CHUNK if t >= REC_CHUNK else REC_SUB
    t_pad = -(-t // chunk) * chunk
    p = p.reshape(depth, n, p.shape[-1])
    h = x.reshape(n, d)
    new_gla, new_hgrn = [], []

    def seqs(a):
        a = a.reshape(bsz, t, a.shape[1])
        return a if t_pad == t else jnp.pad(a, ((0, 0), (0, t_pad - t), (0, 0)))

    for li in range(depth):
        j = li // 2
        kind = "gla" if li % 2 == 0 else "hgrn"
        heads = gla_heads if kind == "gla" else hgrn_heads
        h, q, k, v, g, og = _pre(h, kind, w["ffn1"], (w["mix_norm"],) + w[kind + "_in"], li, j,
                                 min(PRE_TILE[kind], n))
        scale = (q.shape[1] // heads) ** -0.5 if kind == "gla" else 1.0
        s0 = st_gla if kind == "gla" else st_hgrn
        o, s = _rec(seqs(q), seqs(k), seqs(v), seqs(g), None if s0 is None else s0[j], heads, chunk,
                    min(REC_SUB, chunk), scale)
        (new_gla if kind == "gla" else new_hgrn).append(s)
        if t_pad != t:
            o = o[:, :t]
        h = _post(o.reshape(n, o.shape[-1]), og, h, p, w[kind + "_out"], w["ffn2"], w["ple"], li, j, heads,
                  li == depth - 1, tm)
    return h.reshape(bsz, t, d), jnp.stack(new_gla), jnp.stack(new_hgrn)


def kernel(x_prompt, x_sample, state_gla, state_hgrn, p_prompt, p_sample, ffn1_norm, ffn1_w_gu, ffn1_w_down, mix_norm, gla_w_in, gla_w_gk_up, gla_b_gk, gla_gn, gla_w_out, hgrn_w_in, hgrn_gn, hgrn_w_out, hgrn_lower_bounds, ffn2_norm, ffn2_w_gu, ffn2_w_down, ple_norm, ple_w_gate, ple_w_proj, final_norm):
    w = _prep_weights(ffn1_norm, ffn1_w_gu, ffn1_w_down, mix_norm, gla_w_in, gla_w_gk_up, gla_b_gk, gla_gn,
                      gla_w_out, hgrn_w_in, hgrn_gn, hgrn_w_out, hgrn_lower_bounds, ffn2_norm, ffn2_w_gu,
                      ffn2_w_down, ple_norm, ple_w_gate, ple_w_proj, final_norm)
    gla_heads = state_gla.shape[2]
    hgrn_heads = state_hgrn.shape[2]
    y_p, gla_p, hgrn_p = _trunk(x_prompt, p_prompt, None, None, w, gla_heads, hgrn_heads)
    y_s, gla_s, hgrn_s = _trunk(x_sample, p_sample, state_gla, state_hgrn, w, gla_heads, hgrn_heads)
    return (y_p, y_s, gla_p, gla_s, hgrn_p, hgrn_s)
```

```python
import functools

import jax
import jax.numpy as jnp
from jax import lax
from jax.experimental import pallas as pl
from jax.experimental.pallas import tpu as pltpu

F32 = jnp.float32
BF16 = jnp.bfloat16
RMS_EPS = 1e-6
GLA_GATE_NORM = 16.0
LANES = 128
BF16_ROWS = 16
FF_CHUNK = 256
ROW_TILE = 512
PRE_TILE = {"gla": 512, "hgrn": 512}
REC_CHUNK = 64
REC_SUB = 16
REC_SEQS = 8
REC_SEQS_SINGLE = 16
SUBLANES = 8
VMEM_LIMIT = 62 * 1024 * 1024

_NT = (((1,), (1,)), ((), ()))
_TN = (((0,), (0,)), ((), ()))


def _cparams(n_axes):
    return pltpu.CompilerParams(dimension_semantics=("arbitrary",) * n_axes,
                                vmem_limit_bytes=VMEM_LIMIT)


def _resident(shape):
    nd = len(shape)
    return pl.BlockSpec(shape, lambda *_: (0,) * nd, pipeline_mode=pl.Buffered(1))


def _layer(arr, li):
    nd = arr.ndim - 1
    return pl.BlockSpec((None,) + arr.shape[1:], lambda *_: (li,) + (0,) * nd,
                        pipeline_mode=pl.Buffered(1))


def _rows(tm, width):
    return pl.BlockSpec((tm, width), lambda i: (i, 0))


def _dot(a, b, dims=None):
    if dims is None:
        return jnp.dot(a, b, preferred_element_type=F32)
    return lax.dot_general(a, b, dims, preferred_element_type=F32)


def _rms(x, w):
    ms = jnp.mean(x * x, axis=-1, keepdims=True)
    return x * lax.rsqrt(ms + RMS_EPS) * w


def _sigmoid(x):
    return 1.0 / (1.0 + jnp.exp(-x))


def _silu(x):
    return x * _sigmoid(x)


def _ffn_acc(x, nw_ref, wgu_ref, wd_ref, xn_ref, acc_ref):
    d_ff = wd_ref.shape[0]
    xn_ref[...] = _rms(x, nw_ref[...]).astype(BF16)
    for c in range(d_ff // FF_CHUNK):
        lo, hi = c * FF_CHUNK, (c + 1) * FF_CHUNK
        xn = xn_ref[...]
        g = _dot(xn, wgu_ref[:, lo:hi])
        u = _dot(xn, wgu_ref[:, d_ff + lo:d_ff + hi])
        part = _dot((_silu(g) * u).astype(BF16), wd_ref[lo:hi, :])
        if c == 0:
            acc_ref[...] = part
        else:
            acc_ref[...] += part


def _gla_proj(u, w_ref, wr_ref, wup_ref, bgk_ref, kd, vd):
    q = _dot(u, w_ref[:, 0:kd])
    k = _dot(u, w_ref[:, kd:2 * kd])
    v = _dot(u, w_ref[:, 2 * kd:2 * kd + vd])
    og = _dot(u, w_ref[:, 2 * kd + vd:2 * kd + 2 * vd])
    r = _dot(u, wr_ref[...]).astype(BF16)
    x = _dot(r, wup_ref[...]) + bgk_ref[...]
    g = (jnp.minimum(x, 0.0) - jnp.log1p(jnp.exp(-jnp.abs(x)))) * (1.0 / GLA_GATE_NORM)
    return q, k, v, g, og


def _hgrn_proj(u, w_ref, lbw_ref, fd, layer):
    lbw = lbw_ref[...]
    rows = [lbw[i:i + 1, :] for i in range(lbw.shape[0])]
    m = functools.reduce(jnp.maximum, rows)
    es = [jnp.exp(r - m) for r in rows]
    tot = functools.reduce(jnp.add, es)
    sm = [e / tot for e in es]
    lb = functools.reduce(jnp.add, sm[:layer + 1]) - sm[0]

    q = _silu(_dot(u, w_ref[:, 0:fd]))
    f = _dot(u, w_ref[:, fd:2 * fd])
    e = jnp.exp(-jnp.abs(f))
    inv = 1.0 / (1.0 + e)
    pos = f >= 0.0
    sig_f = jnp.where(pos, inv, e * inv)
    sig_nf = jnp.where(pos, e * inv, inv)
    g = jnp.log(lb + (1.0 - lb) * sig_f)
    k = (1.0 - lb) * sig_nf
    v = _dot(u, w_ref[:, 2 * fd:3 * fd])
    og = _dot(u, w_ref[:, 3 * fd:4 * fd])
    return q, k, v, g, og


def _pre_kernel(*refs, kind, hk, hv, layer):
    x_ref, fnw_ref, wgu_ref, wd_ref, nw_ref, w_ref = refs[:6]
    if kind == "gla":
        wr_ref, wup_ref, bgk_ref = refs[6:9]
        rest = refs[9:]
    else:
        lbw_ref = refs[6]
        rest = refs[7:]
    h_ref, q_ref, k_ref, v_ref, g_ref, og_ref, xn_ref, acc_ref = rest
    x = x_ref[...]
    _ffn_acc(x, fnw_ref, wgu_ref, wd_ref, xn_ref, acc_ref)
    h = x + 0.5 * acc_ref[...]
    h_ref[...] = h
    u = _rms(h, nw_ref[...]).astype(BF16)
    if kind == "gla":
        q, k, v, g, og = _gla_proj(u, w_ref, wr_ref, wup_ref, bgk_ref, hk, hv)
    else:
        q, k, v, g, og = _hgrn_proj(u, w_ref, lbw_ref, hk, layer)
    q_ref[...] = q
    k_ref[...] = k
    v_ref[...] = v.astype(BF16)
    g_ref[...] = g
    og_ref[...] = og


def _pre(x, kind, ffn, params, li, j, tm):
    n, d = x.shape
    fnw, wgu, wd = ffn
    if kind == "gla":
        nw, w, wr, wup, bgk = params
        hk, hv = wup.shape[2], (w.shape[2] - 2 * wup.shape[2]) // 2
        in_specs = [_layer(nw, li), _layer(w, j), _layer(wr, j), _layer(wup, j), _layer(bgk, j)]
    else:
        nw, w, lbw = params
        hk = hv = w.shape[2] // 4
        in_specs = [_layer(nw, li), _layer(w, j), _resident(lbw.shape)]
    outs = [((n, d), F32), ((n, hk), F32), ((n, hk), F32), ((n, hv), BF16), ((n, hk), F32), ((n, hv), F32)]
    return pl.pallas_call(
        functools.partial(_pre_kernel, kind=kind, hk=hk, hv=hv, layer=li),
        grid=(n // tm,),
        in_specs=[_rows(tm, d), _layer(fnw, li), _layer(wgu, li), _layer(wd, li)] + in_specs,
        out_specs=[_rows(tm, s[0][1]) for s in outs],
        out_shape=[jax.ShapeDtypeStruct(*s) for s in outs],
        scratch_shapes=[pltpu.VMEM((tm, d), BF16), pltpu.VMEM((tm, d), F32)],
        compiler_params=_cparams(1),
        name="pre_" + kind,
    )(x, fnw, wgu, wd, *params)


def _split3(x):
    x1 = x.astype(BF16)
    r1 = x - x1.astype(F32)
    x2 = r1.astype(BF16)
    x3 = (r1 - x2.astype(F32)).astype(BF16)
    return x1, x2, x3


def _rec_kernel(*refs, heads, kdim, vdim, chunk, sub, scale, single, t_real):
    refs = list(refs)
    q_ref, k_ref, v_ref, g_ref = refs[:4]
    if single:
        s0_ref, o_ref, s_ref, vb_s = refs[4:8]
        scratch = refs[8:]
    else:
        o_ref, s_ref, st_ref = refs[4:7]
        scratch = refs[7:]
    c, sb = chunk, sub
    n_sub = c // sb
    if n_sub > 1:
        qx_s, scratch = scratch[0], scratch[1:]
    qhat_s, kdec_s, kvar_s, dec_s, attn_s = scratch
    if n_sub == 1:
        qx_s = qhat_s
    n = pl.program_id(1)
    hk = heads * kdim

    if single:
        n_seq = q_ref.shape[0] // t_real
        per_tile = SUBLANES // t_real
        pad_rows = jnp.zeros((c - SUBLANES, 1), F32)
        v_rows = v_ref[...].astype(F32)

        def seq_rows(ref, s):
            tile, slot = divmod(s, per_tile)
            x = ref[tile * SUBLANES:(tile + 1) * SUBLANES, :].astype(F32)
            if slot:
                x = pltpu.roll(x, SUBLANES - slot * t_real, 0)
            keep = lax.broadcasted_iota(jnp.int32, x.shape, 0) < t_real
            return jnp.concatenate([jnp.where(keep, x, 0.0), jnp.broadcast_to(pad_rows, (c - SUBLANES, x.shape[1]))],
                                   axis=0)
    else:
        n_seq = q_ref.shape[0]
        seq_rows = lambda ref, s: ref[s]

        @pl.when(n == 0)
        def _init():
            st_ref[...] = jnp.zeros_like(st_ref)

    causal = (lax.broadcasted_iota(jnp.int32, (c, c), 0) >= lax.broadcasted_iota(jnp.int32, (c, c), 1))
    tri = causal.astype(F32).astype(BF16)
    tri3 = jnp.concatenate([tri, tri, tri], axis=1)

    def cumsum(s):
        return _dot(tri3, jnp.concatenate(_split3(seq_rows(g_ref, s)), axis=0))

    def prepare(s, b):
        q = seq_rows(q_ref, s) * scale
        k = seq_rows(k_ref, s)
        if single:
            vb_s[s] = seq_rows(v_rows, s).astype(BF16)
        blk_rows = [slice(i * sb, (i + 1) * sb) for i in range(n_sub)]
        eb = [jnp.exp(b[r] if i == 0 else b[r] - b[i * sb - 1:i * sb, :]) for i, r in enumerate(blk_rows)]
        blk_dec = [e[sb - 1:sb, :] for e in eb]

        def span(lo, hi):
            out = None
            for i in range(lo, hi):
                out = blk_dec[i] if out is None else out * blk_dec[i]
            return out

        mul = lambda x, d: x if d is None else x * d
        qx = [q[r] * eb[i] for i, r in enumerate(blk_rows)]
        kdiag = [k[r] * (1.0 / eb[i]) for i, r in enumerate(blk_rows)]
        kend = [kdiag[i] * blk_dec[i] for i in range(n_sub)]
        cat = lambda parts: parts[0] if len(parts) == 1 else jnp.concatenate(parts, axis=0)
        qhat_s[s] = cat([mul(qx[i], span(0, i)) for i in range(n_sub)]).astype(BF16)
        kdec_s[s] = cat([mul(kend[i], span(i + 1, n_sub)) for i in range(n_sub)]).astype(BF16)
        if n_sub > 1:
            qx_s[s] = cat(qx).astype(BF16)
        for i in range(n_sub):
            parts = [mul(kend[j], span(j + 1, i)) for j in range(i)] + [kdiag[i]]
            kv = cat(parts).astype(BF16)
            if i + 1 < n_sub:
                kv = jnp.concatenate([kv, jnp.zeros((c - (i + 1) * sb, hk), BF16)], axis=0)
            kvar_s[s, i] = kv
        if single:
            p1, p2, p3 = (p.astype(F32) for p in _split3(b[c - 1:c, :]))
            r = lax.broadcasted_iota(jnp.int32, (BF16_ROWS, hk), 0)
            dec_s[s] = jnp.where(r == 0, p1, jnp.where(r == 1, p2, jnp.where(r == 2, p3, 0.0))).astype(BF16)
        else:
            dec_s[s] = span(0, n_sub)

    def scores(s, h):
        ks = slice(h * kdim, (h + 1) * kdim)
        parts = [_dot(qx_s[s, blk * sb:(blk + 1) * sb, ks], kvar_s[s, blk, :, ks], _NT)
                 for blk in range(n_sub)]
        attn = parts[0] if n_sub == 1 else jnp.concatenate(parts, axis=0)
        attn_s[s, h] = jnp.where(causal, attn, 0.0).astype(BF16)

    def update(s, h):
        ks = slice(h * kdim, (h + 1) * kdim)
        vs = slice(h * vdim, (h + 1) * vdim)
        vh = vb_s[s, :, vs] if single else v_ref[s, :, vs]
        o_intra = _dot(attn_s[s, h], vh)
        if single:
            st = s0_ref[s, h]
            ones = jnp.ones((BF16_ROWS, LANES), BF16)
            dec = jnp.exp(_dot(dec_s[s, :, ks], ones, _TN))
            dec = jnp.concatenate([dec] * (vdim // LANES), axis=1)
            s_ref[s, h] = st * dec + _dot(kdec_s[s, :, ks], vh, _TN)
            return o_intra + _dot(qhat_s[s, :, ks], st.astype(BF16))
        else:
            st = st_ref[s, h]
            o_ref[s, :, vs] = o_intra + _dot(qhat_s[s, :, ks], st.astype(BF16), _NT)
            st_ref[s, h] = st * dec_s[s, :, ks] + _dot(vh, kdec_s[s, :, ks], _TN)

    pairs = [(s, h) for h in range(heads) for s in range(n_seq)]
    bs = [cumsum(s) for s in range(n_seq)]
    for s in range(n_seq):
        prepare(s, bs[s])
    for s, h in pairs:
        scores(s, h)
    if single:
        row = lax.broadcasted_iota(jnp.int32, (SUBLANES, vdim), 0)
        for h in range(heads):
            for tile in range(n_seq // per_tile):
                merged = None
                for slot in range(per_tile):
                    o = jnp.where(row < t_real, update(tile * per_tile + slot, h)[:SUBLANES], 0.0)
                    if slot:
                        o = pltpu.roll(o, slot * t_real, 0)
                    merged = o if merged is None else merged + o
                o_ref[tile * SUBLANES:(tile + 1) * SUBLANES, h * vdim:(h + 1) * vdim] = merged
    else:
        for s, h in pairs:
            update(s, h)

    if not single:
        @pl.when(n == pl.num_programs(1) - 1)
        def _final():
            for s, h in pairs:
                s_ref[s, h] = st_ref[s, h].T


def _rec(q, k, v, g, s0, n_seq, heads, scale):
    hk, hv = q.shape[1], v.shape[1]
    t = q.shape[0] // n_seq
    kdim, vdim = hk // heads, hv // heads
    single = s0 is not None
    bb = min(REC_SEQS_SINGLE if single else REC_SEQS, n_seq)
    chunk = REC_SUB if single else REC_CHUNK
    sub = REC_SUB
    n_sub = chunk // sub
    state = pl.BlockSpec((bb, heads, kdim, vdim), lambda i, j: (i, 0, 0, 0))
    if single:
        assert SUBLANES % t == 0 and bb % (SUBLANES // t) == 0
        grid = (n_seq // bb, 1)
        tok = lambda w: pl.BlockSpec((bb * t, w), lambda i, j: (i, 0))
        args = [q, k, v, g, s0]
        in_specs = [tok(hk), tok(hk), tok(hv), tok(hk), state]
        o_shape = (n_seq * t, hv)
    else:
        assert t % chunk == 0
        grid = (n_seq // bb, t // chunk)
        tok = lambda w: pl.BlockSpec((bb, chunk, w), lambda i, j: (i, j, 0))
        args = [a.reshape(n_seq, t, a.shape[1]) for a in (q, k, v, g)]
        in_specs = [tok(hk), tok(hk), tok(hv), tok(hk)]
        o_shape = (n_seq, t, hv)
    scratch = []
    if single:
        scratch.append(pltpu.VMEM((bb, chunk, hv), BF16))
    else:
        scratch.append(pltpu.VMEM((bb, heads, vdim, kdim), F32))
    if n_sub > 1:
        scratch.append(pltpu.VMEM((bb, chunk, hk), BF16))
    scratch += [
        pltpu.VMEM((bb, chunk, hk), BF16),
        pltpu.VMEM((bb, chunk, hk), BF16),
        pltpu.VMEM((bb, n_sub, chunk, hk), BF16),
        pltpu.VMEM((bb, BF16_ROWS, hk), BF16) if single else pltpu.VMEM((bb, 1, hk), F32),
        pltpu.VMEM((bb, heads, chunk, chunk), BF16),
    ]
    o, s = pl.pallas_call(
        functools.partial(_rec_kernel, heads=heads, kdim=kdim, vdim=vdim, chunk=chunk, sub=sub,
                          scale=scale, single=single, t_real=t),
        grid=grid,
        in_specs=in_specs,
        out_specs=[tok(hv), state],
        out_shape=[jax.ShapeDtypeStruct(o_shape, F32),
                   jax.ShapeDtypeStruct((n_seq, heads, kdim, vdim), F32)],
        scratch_shapes=scratch,
        compiler_params=_cparams(2),
        name="rec",
    )(*args)
    return o.reshape(n_seq * t, hv), s


def _post_kernel(o_ref, og_ref, h_ref, p_ref, gn_ref, wo_ref, nw_ref, wgu_ref, wd_ref,
                 pn_ref, pwg_ref, pwp_ref, fn_ref, out_ref, xn_ref, acc_ref, *, heads, final):
    o = o_ref[...]
    vdim = o.shape[1] // heads
    gn = gn_ref[...]
    y = jnp.concatenate([_rms(o[:, h * vdim:(h + 1) * vdim], gn) for h in range(heads)], axis=1)
    y = (y * _silu(og_ref[...])).astype(BF16)
    out_ref[...] = h_ref[...] + _dot(y, wo_ref[...])
    _ffn_acc(out_ref[...], nw_ref, wgu_ref, wd_ref, xn_ref, acc_ref)
    h2 = out_ref[...] + 0.5 * acc_ref[...]
    gate = _sigmoid(_dot(_rms(h2, pn_ref[...]).astype(BF16), pwg_ref[...]))
    h3 = h2 + gate * _dot(p_ref[...].astype(BF16), pwp_ref[...])
    if final:
        h3 = _rms(h3, fn_ref[...])
    out_ref[...] = h3


def _post(o, og, h, p, mix_out, ffn, ple, li, j, heads, final, tm):
    n, hv = o.shape
    d = h.shape[1]
    gn, wo = mix_out
    nw, wgu, wd = ffn
    pn, pwg, pwp, fn = ple
    return pl.pallas_call(
        functools.partial(_post_kernel, heads=heads, final=final),
        grid=(n // tm,),
        in_specs=[_rows(tm, hv), _rows(tm, hv), _rows(tm, d),
                  pl.BlockSpec((None, tm, p.shape[2]), lambda i: (li, i, 0)),
                  _layer(gn, j), _layer(wo, j), _layer(nw, li), _layer(wgu, li), _layer(wd, li),
                  _layer(pn, li), _layer(pwg, li), _layer(pwp, li), _resident(fn.shape)],
        out_specs=_rows(tm, d),
        out_shape=jax.ShapeDtypeStruct((n, d), F32),
        scratch_shapes=[pltpu.VMEM((tm, d), BF16), pltpu.VMEM((tm, d), F32)],
        compiler_params=_cparams(1),
        name="post",
    )(o, og, h, p, gn, wo, nw, wgu, wd, pn, pwg, pwp, fn)


def _prep_weights(ffn1_norm, ffn1_w_gu, ffn1_w_down, mix_norm, gla_w_in, gla_w_gk_up, gla_b_gk, gla_gn,
                  gla_w_out, hgrn_w_in, hgrn_gn, hgrn_w_out, hgrn_lower_bounds, ffn2_norm, ffn2_w_gu,
                  ffn2_w_down, ple_norm, ple_w_gate, ple_w_proj, final_norm):
    row = lambda a: a.reshape(a.shape[0], 1, a.shape[1])
    rank = gla_w_gk_up.shape[1]
    main = gla_w_in.shape[2] - rank
    return dict(
        ffn1=(row(ffn1_norm), ffn1_w_gu.astype(BF16), ffn1_w_down.astype(BF16)),
        ffn2=(row(ffn2_norm), ffn2_w_gu.astype(BF16), ffn2_w_down.astype(BF16)),
        mix_norm=row(mix_norm),
        gla_in=(gla_w_in[:, :, :main].astype(BF16),
                jnp.pad(gla_w_in[:, :, main:], ((0, 0), (0, 0), (0, LANES - rank))).astype(BF16),
                jnp.pad(gla_w_gk_up, ((0, 0), (0, LANES - rank), (0, 0))).astype(BF16),
                row(gla_b_gk)),
        gla_out=(row(gla_gn), gla_w_out.astype(BF16)),
        hgrn_in=(hgrn_w_in.astype(BF16), hgrn_lower_bounds),
        hgrn_out=(row(hgrn_gn), hgrn_w_out.astype(BF16)),
        ple=(row(ple_norm), ple_w_gate.astype(BF16), ple_w_proj.astype(BF16), final_norm.reshape(1, -1)),
    )


def _trunk(x, p, st_gla, st_hgrn, w, gla_heads, hgrn_heads):
    bsz, t, d = x.shape
    n = bsz * t
    depth = p.shape[0]
    tm = min(ROW_TILE, n)
    p = p.reshape(depth, n, p.shape[-1])
    h = x.reshape(n, d)
    new_gla, new_hgrn = [], []
    for li in range(depth):
        j = li // 2
        kind = "gla" if li % 2 == 0 else "hgrn"
        heads = gla_heads if kind == "gla" else hgrn_heads
        h, q, k, v, g, og = _pre(h, kind, w["ffn1"], (w["mix_norm"],) + w[kind + "_in"], li, j,
                                 min(PRE_TILE[kind], n))
        scale = (q.shape[1] // heads) ** -0.5 if kind == "gla" else 1.0
        s0 = st_gla if kind == "gla" else st_hgrn
        o, s = _rec(q, k, v, g, None if s0 is None else s0[j], bsz, heads, scale)
        (new_gla if kind == "gla" else new_hgrn).append(s)
        h = _post(o, og, h, p, w[kind + "_out"], w["ffn2"], w["ple"], li, j, heads, li == depth - 1, tm)
    return h.reshape(bsz, t, d), jnp.stack(new_gla), jnp.stack(new_hgrn)


def kernel(x_prompt, x_sample, state_gla, state_hgrn, p_prompt, p_sample, ffn1_norm, ffn1_w_gu, ffn1_w_down, mix_norm, gla_w_in, gla_w_gk_up, gla_b_gk, gla_gn, gla_w_out, hgrn_w_in, hgrn_gn, hgrn_w_out, hgrn_lower_bounds, ffn2_norm, ffn2_w_gu, ffn2_w_down, ple_norm, ple_w_gate, ple_w_proj, final_norm):
    w = _prep_weights(ffn1_norm, ffn1_w_gu, ffn1_w_down, mix_norm, gla_w_in, gla_w_gk_up, gla_b_gk, gla_gn,
                      gla_w_out, hgrn_w_in, hgrn_gn, hgrn_w_out, hgrn_lower_bounds, ffn2_norm, ffn2_w_gu,
                      ffn2_w_down, ple_norm, ple_w_gate, ple_w_proj, final_norm)
    gla_heads = state_gla.shape[2]
    hgrn_heads = state_hgrn.shape[2]
    y_p, gla_p, hgrn_p = _trunk(x_prompt, p_prompt, None, None, w, gla_heads, hgrn_heads)
    y_s, gla_s, hgrn_s = _trunk(x_sample, p_sample, state_gla, state_hgrn, w, gla_heads, hgrn_heads)
    return (y_p, y_s, gla_p, gla_s, hgrn_p, hgrn_s)
```

```python
import functools
import types

import jax
import jax.numpy as jnp
from jax import lax
from jax.experimental import pallas as pl
from jax.experimental.pallas import tpu as pltpu

F32 = jnp.float32
BF16 = jnp.bfloat16
RMS_EPS = 1e-6
GLA_GATE_NORM = 16.0
LANES = 128
BF16_ROWS = 16
FF_CHUNK = 256
ROW_TILE = 512
PRE_TILE = {"gla": 512, "hgrn": 512}
REC_CHUNK = 64
REC_SUB = 16
REC_CHUNKS_PER_STEP = 2
REC_SEQS = 8
REC_SEQS_SINGLE = 16
SUBLANES = 8
VMEM_LIMIT = 62 * 1024 * 1024

_NT = (((1,), (1,)), ((), ()))
_TN = (((0,), (0,)), ((), ()))


def _cparams(n_axes):
    return pltpu.CompilerParams(dimension_semantics=("arbitrary",) * n_axes,
                                vmem_limit_bytes=VMEM_LIMIT)


def _resident(shape):
    nd = len(shape)
    return pl.BlockSpec(shape, lambda *_: (0,) * nd, pipeline_mode=pl.Buffered(1))


def _layer(arr, li):
    nd = arr.ndim - 1
    return pl.BlockSpec((None,) + arr.shape[1:], lambda *_: (li,) + (0,) * nd,
                        pipeline_mode=pl.Buffered(1))


def _rows(tm, width):
    return pl.BlockSpec((tm, width), lambda i: (i, 0))


def _dot(a, b, dims=None):
    if dims is None:
        return jnp.dot(a, b, preferred_element_type=F32)
    return lax.dot_general(a, b, dims, preferred_element_type=F32)


def _rms(x, w):
    ms = jnp.mean(x * x, axis=-1, keepdims=True)
    return x * lax.rsqrt(ms + RMS_EPS) * w


def _sigmoid(x):
    return 1.0 / (1.0 + jnp.exp(-x))


def _silu(x):
    return x * _sigmoid(x)


def _ffn_acc(x, nw_ref, wgu_ref, wd_ref, xn_ref, acc_ref):
    d_ff = wd_ref.shape[0]
    xn_ref[...] = _rms(x, nw_ref[...]).astype(BF16)
    for c in range(d_ff // FF_CHUNK):
        lo, hi = c * FF_CHUNK, (c + 1) * FF_CHUNK
        xn = xn_ref[...]
        g = _dot(xn, wgu_ref[:, lo:hi])
        u = _dot(xn, wgu_ref[:, d_ff + lo:d_ff + hi])
        part = _dot((_silu(g) * u).astype(BF16), wd_ref[lo:hi, :])
        if c == 0:
            acc_ref[...] = part
        else:
            acc_ref[...] += part


def _gla_proj(u, w_ref, wr_ref, wup_ref, bgk_ref, kd, vd):
    q = _dot(u, w_ref[:, 0:kd])
    k = _dot(u, w_ref[:, kd:2 * kd])
    v = _dot(u, w_ref[:, 2 * kd:2 * kd + vd])
    og = _dot(u, w_ref[:, 2 * kd + vd:2 * kd + 2 * vd])
    r = _dot(u, wr_ref[...]).astype(BF16)
    x = _dot(r, wup_ref[...]) + bgk_ref[...]
    g = (jnp.minimum(x, 0.0) - jnp.log1p(jnp.exp(-jnp.abs(x)))) * (1.0 / GLA_GATE_NORM)
    return q, k, v, g, og


def _hgrn_proj(u, w_ref, lbw_ref, fd, layer):
    lbw = lbw_ref[...]
    rows = [lbw[i:i + 1, :] for i in range(lbw.shape[0])]
    m = functools.reduce(jnp.maximum, rows)
    es = [jnp.exp(r - m) for r in rows]
    tot = functools.reduce(jnp.add, es)
    sm = [e / tot for e in es]
    lb = functools.reduce(jnp.add, sm[:layer + 1]) - sm[0]

    q = _silu(_dot(u, w_ref[:, 0:fd]))
    f = _dot(u, w_ref[:, fd:2 * fd])
    e = jnp.exp(-jnp.abs(f))
    inv = 1.0 / (1.0 + e)
    pos = f >= 0.0
    sig_f = jnp.where(pos, inv, e * inv)
    sig_nf = jnp.where(pos, e * inv, inv)
    g = jnp.log(lb + (1.0 - lb) * sig_f)
    k = (1.0 - lb) * sig_nf
    v = _dot(u, w_ref[:, 2 * fd:3 * fd])
    og = _dot(u, w_ref[:, 3 * fd:4 * fd])
    return q, k, v, g, og


def _pre_kernel(*refs, kind, hk, hv, layer):
    x_ref, fnw_ref, wgu_ref, wd_ref, nw_ref, w_ref = refs[:6]
    if kind == "gla":
        wr_ref, wup_ref, bgk_ref = refs[6:9]
        rest = refs[9:]
    else:
        lbw_ref = refs[6]
        rest = refs[7:]
    h_ref, q_ref, k_ref, v_ref, g_ref, og_ref, xn_ref, acc_ref = rest
    x = x_ref[...]
    _ffn_acc(x, fnw_ref, wgu_ref, wd_ref, xn_ref, acc_ref)
    h = x + 0.5 * acc_ref[...]
    h_ref[...] = h
    u = _rms(h, nw_ref[...]).astype(BF16)
    if kind == "gla":
        q, k, v, g, og = _gla_proj(u, w_ref, wr_ref, wup_ref, bgk_ref, hk, hv)
    else:
        q, k, v, g, og = _hgrn_proj(u, w_ref, lbw_ref, hk, layer)
    q_ref[...] = q
    k_ref[...] = k
    v_ref[...] = v.astype(BF16)
    g_ref[...] = g
    og_ref[...] = og


def _pre(x, kind, ffn, params, li, j, tm):
    n, d = x.shape
    fnw, wgu, wd = ffn
    if kind == "gla":
        nw, w, wr, wup, bgk = params
        hk, hv = wup.shape[2], (w.shape[2] - 2 * wup.shape[2]) // 2
        in_specs = [_layer(nw, li), _layer(w, j), _layer(wr, j), _layer(wup, j), _layer(bgk, j)]
    else:
        nw, w, lbw = params
        hk = hv = w.shape[2] // 4
        in_specs = [_layer(nw, li), _layer(w, j), _resident(lbw.shape)]
    outs = [((n, d), F32), ((n, hk), F32), ((n, hk), F32), ((n, hv), BF16), ((n, hk), F32), ((n, hv), F32)]
    return pl.pallas_call(
        functools.partial(_pre_kernel, kind=kind, hk=hk, hv=hv, layer=li),
        grid=(n // tm,),
        in_specs=[_rows(tm, d), _layer(fnw, li), _layer(wgu, li), _layer(wd, li)] + in_specs,
        out_specs=[_rows(tm, s[0][1]) for s in outs],
        out_shape=[jax.ShapeDtypeStruct(*s) for s in outs],
        scratch_shapes=[pltpu.VMEM((tm, d), BF16), pltpu.VMEM((tm, d), F32)],
        compiler_params=_cparams(1),
        name="pre_" + kind,
    )(x, fnw, wgu, wd, *params)


def _split3(x):
    x1 = x.astype(BF16)
    r1 = x - x1.astype(F32)
    x2 = r1.astype(BF16)
    x3 = (r1 - x2.astype(F32)).astype(BF16)
    return x1, x2, x3


def _rec_kernel(*refs, heads, kdim, vdim, chunk, sub, scale, single, t_real):
    refs = list(refs)
    q_ref, k_ref, v_ref, g_ref = refs[:4]
    if single:
        s0_ref, o_ref, s_ref, vb_s = refs[4:8]
        scratch = refs[8:]
    else:
        o_ref, s_ref, st_ref = refs[4:7]
        scratch = refs[7:]
    c, sb = chunk, sub
    n_sub = c // sb
    if n_sub > 1:
        qx_s, scratch = scratch[0], scratch[1:]
    qhat_s, kdec_s, kvar_s, dec_s, attn_s = scratch
    if n_sub == 1:
        qx_s = qhat_s
    n = pl.program_id(1)
    hk = heads * kdim
    cur = types.SimpleNamespace(rows=slice(0, c))

    if single:
        n_seq = q_ref.shape[0] // t_real
        per_tile = SUBLANES // t_real
        pad_rows = jnp.zeros((c - SUBLANES, 1), F32)
        v_rows = v_ref[...].astype(F32)

        def seq_rows(ref, s):
            tile, slot = divmod(s, per_tile)
            x = ref[tile * SUBLANES:(tile + 1) * SUBLANES, :].astype(F32)
            if slot:
                x = pltpu.roll(x, SUBLANES - slot * t_real, 0)
            keep = lax.broadcasted_iota(jnp.int32, x.shape, 0) < t_real
            return jnp.concatenate([jnp.where(keep, x, 0.0), jnp.broadcast_to(pad_rows, (c - SUBLANES, x.shape[1]))],
                                   axis=0)
    else:
        n_seq = q_ref.shape[0]
        seq_rows = lambda ref, s: ref[s, cur.rows]

        @pl.when(n == 0)
        def _init():
            st_ref[...] = jnp.zeros_like(st_ref)

    causal = (lax.broadcasted_iota(jnp.int32, (c, c), 0) >= lax.broadcasted_iota(jnp.int32, (c, c), 1))
    tri = causal.astype(F32).astype(BF16)
    tri3 = jnp.concatenate([tri, tri, tri], axis=1)

    def cumsum(s):
        return _dot(tri3, jnp.concatenate(_split3(seq_rows(g_ref, s)), axis=0))

    def prepare(s, b):
        q = seq_rows(q_ref, s) * scale
        k = seq_rows(k_ref, s)
        if single:
            vb_s[s] = seq_rows(v_rows, s).astype(BF16)
        blk_rows = [slice(i * sb, (i + 1) * sb) for i in range(n_sub)]
        eb = [jnp.exp(b[r] if i == 0 else b[r] - b[i * sb - 1:i * sb, :]) for i, r in enumerate(blk_rows)]
        blk_dec = [e[sb - 1:sb, :] for e in eb]

        def span(lo, hi):
            out = None
            for i in range(lo, hi):
                out = blk_dec[i] if out is None else out * blk_dec[i]
            return out

        mul = lambda x, d: x if d is None else x * d
        qx = [q[r] * eb[i] for i, r in enumerate(blk_rows)]
        kdiag = [k[r] * (1.0 / eb[i]) for i, r in enumerate(blk_rows)]
        kend = [kdiag[i] * blk_dec[i] for i in range(n_sub)]
        cat = lambda parts: parts[0] if len(parts) == 1 else jnp.concatenate(parts, axis=0)
        qhat_s[s] = cat([mul(qx[i], span(0, i)) for i in range(n_sub)]).astype(BF16)
        kdec_s[s] = cat([mul(kend[i], span(i + 1, n_sub)) for i in range(n_sub)]).astype(BF16)
        if n_sub > 1:
            qx_s[s] = cat(qx).astype(BF16)
        for i in range(n_sub):
            parts = [mul(kend[j], span(j + 1, i)) for j in range(i)] + [kdiag[i]]
            kv = cat(parts).astype(BF16)
            if i + 1 < n_sub:
                kv = jnp.concatenate([kv, jnp.zeros((c - (i + 1) * sb, hk), BF16)], axis=0)
            kvar_s[s, i] = kv
        if single:
            p1, p2, p3 = (p.astype(F32) for p in _split3(b[c - 1:c, :]))
            r = lax.broadcasted_iota(jnp.int32, (BF16_ROWS, hk), 0)
            dec_s[s] = jnp.where(r == 0, p1, jnp.where(r == 1, p2, jnp.where(r == 2, p3, 0.0))).astype(BF16)
        else:
            dec_s[s] = span(0, n_sub)

    def scores(s, h):
        ks = slice(h * kdim, (h + 1) * kdim)
        parts = [_dot(qx_s[s, blk * sb:(blk + 1) * sb, ks], kvar_s[s, blk, :, ks], _NT)
                 for blk in range(n_sub)]
        attn = parts[0] if n_sub == 1 else jnp.concatenate(parts, axis=0)
        attn_s[s, h] = jnp.where(causal, attn, 0.0).astype(BF16)

    def update(s, h):
        ks = slice(h * kdim, (h + 1) * kdim)
        vs = slice(h * vdim, (h + 1) * vdim)
        vh = vb_s[s, :, vs] if single else v_ref[s, cur.rows, vs]
        o_intra = _dot(attn_s[s, h], vh)
        if single:
            st = s0_ref[s, h]
            ones = jnp.ones((BF16_ROWS, LANES), BF16)
            dec = jnp.exp(_dot(dec_s[s, :, ks], ones, _TN))
            dec = jnp.concatenate([dec] * (vdim // LANES), axis=1)
            s_ref[s, h] = st * dec + _dot(kdec_s[s, :, ks], vh, _TN)
            return o_intra + _dot(qhat_s[s, :, ks], st.astype(BF16))
        else:
            st = st_ref[s, h]
            o_ref[s, cur.rows, vs] = o_intra + _dot(qhat_s[s, :, ks], st.astype(BF16), _NT)
            st_ref[s, h] = st * dec_s[s, :, ks] + _dot(vh, kdec_s[s, :, ks], _TN)

    pairs = [(s, h) for h in range(heads) for s in range(n_seq)]

    def stages():
        bs = [cumsum(s) for s in range(n_seq)]
        for s in range(n_seq):
            prepare(s, bs[s])
        for s, h in pairs:
            scores(s, h)

    if single:
        stages()
        row = lax.broadcasted_iota(jnp.int32, (SUBLANES, vdim), 0)
        for h in range(heads):
            for tile in range(n_seq // per_tile):
                merged = None
                for slot in range(per_tile):
                    o = jnp.where(row < t_real, update(tile * per_tile + slot, h)[:SUBLANES], 0.0)
                    if slot:
                        o = pltpu.roll(o, slot * t_real, 0)
                    merged = o if merged is None else merged + o
                o_ref[tile * SUBLANES:(tile + 1) * SUBLANES, h * vdim:(h + 1) * vdim] = merged
    else:
        for lo in range(0, q_ref.shape[1], c):
            cur.rows = slice(lo, lo + c)
            stages()
            for s, h in pairs:
                update(s, h)

        @pl.when(n == pl.num_programs(1) - 1)
        def _final():
            for s, h in pairs:
                s_ref[s, h] = st_ref[s, h].T


def _rec(q, k, v, g, s0, n_seq, heads, scale):
    hk, hv = q.shape[1], v.shape[1]
    t = q.shape[0] // n_seq
    kdim, vdim = hk // heads, hv // heads
    single = s0 is not None
    bb = min(REC_SEQS_SINGLE if single else REC_SEQS, n_seq)
    chunk = REC_SUB if single else REC_CHUNK
    sub = REC_SUB
    n_sub = chunk // sub
    state = pl.BlockSpec((bb, heads, kdim, vdim), lambda i, j: (i, 0, 0, 0))
    if single:
        assert SUBLANES % t == 0 and bb % (SUBLANES // t) == 0
        grid = (n_seq // bb, 1)
        tok = lambda w: pl.BlockSpec((bb * t, w), lambda i, j: (i, 0))
        args = [q, k, v, g, s0]
        in_specs = [tok(hk), tok(hk), tok(hv), tok(hk), state]
        o_shape = (n_seq * t, hv)
    else:
        step_rows = REC_CHUNKS_PER_STEP * chunk
        assert t % step_rows == 0
        grid = (n_seq // bb, t // step_rows)
        tok = lambda w: pl.BlockSpec((bb, step_rows, w), lambda i, j: (i, j, 0))
        args = [a.reshape(n_seq, t, a.shape[1]) for a in (q, k, v, g)]
        in_specs = [tok(hk), tok(hk), tok(hv), tok(hk)]
        o_shape = (n_seq, t, hv)
    scratch = []
    if single:
        scratch.append(pltpu.VMEM((bb, chunk, hv), BF16))
    else:
        scratch.append(pltpu.VMEM((bb, heads, vdim, kdim), F32))
    if n_sub > 1:
        scratch.append(pltpu.VMEM((bb, chunk, hk), BF16))
    scratch += [
        pltpu.VMEM((bb, chunk, hk), BF16),
        pltpu.VMEM((bb, chunk, hk), BF16),
        pltpu.VMEM((bb, n_sub, chunk, hk), BF16),
        pltpu.VMEM((bb, BF16_ROWS, hk), BF16) if single else pltpu.VMEM((bb, 1, hk), F32),
        pltpu.VMEM((bb, heads, chunk, chunk), BF16),
    ]
    o, s = pl.pallas_call(
        functools.partial(_rec_kernel, heads=heads, kdim=kdim, vdim=vdim, chunk=chunk, sub=sub,
                          scale=scale, single=single, t_real=t),
        grid=grid,
        in_specs=in_specs,
        out_specs=[tok(hv), state],
        out_shape=[jax.ShapeDtypeStruct(o_shape, F32),
                   jax.ShapeDtypeStruct((n_seq, heads, kdim, vdim), F32)],
        scratch_shapes=scratch,
        compiler_params=_cparams(2),
        name="rec",
    )(*args)
    return o.reshape(n_seq * t, hv), s


def _post_kernel(o_ref, og_ref, h_ref, p_ref, gn_ref, wo_ref, nw_ref, wgu_ref, wd_ref,
                 pn_ref, pwg_ref, pwp_ref, fn_ref, out_ref, xn_ref, acc_ref, *, heads, final):
    o = o_ref[...]
    vdim = o.shape[1] // heads
    gn = gn_ref[...]
    y = jnp.concatenate([_rms(o[:, h * vdim:(h + 1) * vdim], gn) for h in range(heads)], axis=1)
    y = (y * _silu(og_ref[...])).astype(BF16)
    out_ref[...] = h_ref[...] + _dot(y, wo_ref[...])
    _ffn_acc(out_ref[...], nw_ref, wgu_ref, wd_ref, xn_ref, acc_ref)
    h2 = out_ref[...] + 0.5 * acc_ref[...]
    gate = _sigmoid(_dot(_rms(h2, pn_ref[...]).astype(BF16), pwg_ref[...]))
    h3 = h2 + gate * _dot(p_ref[...].astype(BF16), pwp_ref[...])
    if final:
        h3 = _rms(h3, fn_ref[...])
    out_ref[...] = h3


def _post(o, og, h, p, mix_out, ffn, ple, li, j, heads, final, tm):
    n, hv = o.shape
    d = h.shape[1]
    gn, wo = mix_out
    nw, wgu, wd = ffn
    pn, pwg, pwp, fn = ple
    return pl.pallas_call(
        functools.partial(_post_kernel, heads=heads, final=final),
        grid=(n // tm,),
        in_specs=[_rows(tm, hv), _rows(tm, hv), _rows(tm, d),
                  pl.BlockSpec((None, tm, p.shape[2]), lambda i: (li, i, 0)),
                  _layer(gn, j), _layer(wo, j), _layer(nw, li), _layer(wgu, li), _layer(wd, li),
                  _layer(pn, li), _layer(pwg, li), _layer(pwp, li), _resident(fn.shape)],
        out_specs=_rows(tm, d),
        out_shape=jax.ShapeDtypeStruct((n, d), F32),
        scratch_shapes=[pltpu.VMEM((tm, d), BF16), pltpu.VMEM((tm, d), F32)],
        compiler_params=_cparams(1),
        name="post",
    )(o, og, h, p, gn, wo, nw, wgu, wd, pn, pwg, pwp, fn)


def _prep_weights(ffn1_norm, ffn1_w_gu, ffn1_w_down, mix_norm, gla_w_in, gla_w_gk_up, gla_b_gk, gla_gn,
                  gla_w_out, hgrn_w_in, hgrn_gn, hgrn_w_out, hgrn_lower_bounds, ffn2_norm, ffn2_w_gu,
                  ffn2_w_down, ple_norm, ple_w_gate, ple_w_proj, final_norm):
    row = lambda a: a.reshape(a.shape[0], 1, a.shape[1])
    rank = gla_w_gk_up.shape[1]
    main = gla_w_in.shape[2] - rank
    return dict(
        ffn1=(row(ffn1_norm), ffn1_w_gu.astype(BF16), ffn1_w_down.astype(BF16)),
        ffn2=(row(ffn2_norm), ffn2_w_gu.astype(BF16), ffn2_w_down.astype(BF16)),
        mix_norm=row(mix_norm),
        gla_in=(gla_w_in[:, :, :main].astype(BF16),
                jnp.pad(gla_w_in[:, :, main:], ((0, 0), (0, 0), (0, LANES - rank))).astype(BF16),
                jnp.pad(gla_w_gk_up, ((0, 0), (0, LANES - rank), (0, 0))).astype(BF16),
                row(gla_b_gk)),
        gla_out=(row(gla_gn), gla_w_out.astype(BF16)),
        hgrn_in=(hgrn_w_in.astype(BF16), hgrn_lower_bounds),
        hgrn_out=(row(hgrn_gn), hgrn_w_out.astype(BF16)),
        ple=(row(ple_norm), ple_w_gate.astype(BF16), ple_w_proj.astype(BF16), final_norm.reshape(1, -1)),
    )


def _trunk(x, p, st_gla, st_hgrn, w, gla_heads, hgrn_heads):
    bsz, t, d = x.shape
    n = bsz * t
    depth = p.shape[0]
    tm = min(ROW_TILE, n)
    p = p.reshape(depth, n, p.shape[-1])
    h = x.reshape(n, d)
    new_gla, new_hgrn = [], []
    for li in range(depth):
        j = li // 2
        kind = "gla" if li % 2 == 0 else "hgrn"
        heads = gla_heads if kind == "gla" else hgrn_heads
        h, q, k, v, g, og = _pre(h, kind, w["ffn1"], (w["mix_norm"],) + w[kind + "_in"], li, j,
                                 min(PRE_TILE[kind], n))
        scale = (q.shape[1] // heads) ** -0.5 if kind == "gla" else 1.0
        s0 = st_gla if kind == "gla" else st_hgrn
        o, s = _rec(q, k, v, g, None if s0 is None else s0[j], bsz, heads, scale)
        (new_gla if kind == "gla" else new_hgrn).append(s)
        h = _post(o, og, h, p, w[kind + "_out"], w["ffn2"], w["ple"], li, j, heads, li == depth - 1, tm)
    return h.reshape(bsz, t, d), jnp.stack(new_gla), jnp.stack(new_hgrn)


def kernel(x_prompt, x_sample, state_gla, state_hgrn, p_prompt, p_sample, ffn1_norm, ffn1_w_gu, ffn1_w_down, mix_norm, gla_w_in, gla_w_gk_up, gla_b_gk, gla_gn, gla_w_out, hgrn_w_in, hgrn_gn, hgrn_w_out, hgrn_lower_bounds, ffn2_norm, ffn2_w_gu, ffn2_w_down, ple_norm, ple_w_gate, ple_w_proj, final_norm):
    w = _prep_weights(ffn1_norm, ffn1_w_gu, ffn1_w_down, mix_norm, gla_w_in, gla_w_gk_up, gla_b_gk, gla_gn,
                      gla_w_out, hgrn_w_in, hgrn_gn, hgrn_w_out, hgrn_lower_bounds, ffn2_norm, ffn2_w_gu,
                      ffn2_w_down, ple_norm, ple_w_gate, ple_w_proj, final_norm)
    gla_heads = state_gla.shape[2]
    hgrn_heads = state_hgrn.shape[2]
    y_p, gla_p, hgrn_p = _trunk(x_prompt, p_prompt, None, None, w, gla_heads, hgrn_heads)
    y_s, gla_s, hgrn_s = _trunk(x_sample, p_sample, state_gla, state_hgrn, w, gla_heads, hgrn_heads)
    return (y_p, y_s, gla_p, gla_s, hgrn_p, hgrn_s)
```

```python
import functools
import types

import jax
import jax.numpy as jnp
from jax import lax
from jax.experimental import pallas as pl
from jax.experimental.pallas import tpu as pltpu

F32 = jnp.float32
BF16 = jnp.bfloat16
RMS_EPS = 1e-6
GLA_GATE_NORM = 16.0
LANES = 128
BF16_ROWS = 16
FF_CHUNK = 256
ROW_TILE = 512
REC_CHUNK = 64
REC_SUB = 16
REC_CHUNKS_PER_STEP = 2
REC_SEQS = 8
REC_SEQS_SINGLE = 16
SUBLANES = 8
VMEM_LIMIT = 62 * 1024 * 1024

_NT = (((1,), (1,)), ((), ()))
_TN = (((0,), (0,)), ((), ()))


def _cparams(n_axes):
    return pltpu.CompilerParams(dimension_semantics=("arbitrary",) * n_axes,
                                vmem_limit_bytes=VMEM_LIMIT)


def _resident(shape):
    nd = len(shape)
    return pl.BlockSpec(shape, lambda *_: (0,) * nd, pipeline_mode=pl.Buffered(1))


def _layer(arr, li):
    nd = arr.ndim - 1
    return pl.BlockSpec((None,) + arr.shape[1:], lambda *_: (li,) + (0,) * nd,
                        pipeline_mode=pl.Buffered(1))


def _rows(tm, width):
    return pl.BlockSpec((tm, width), lambda i: (i, 0))


def _dot(a, b, dims=None):
    if dims is None:
        return jnp.dot(a, b, preferred_element_type=F32)
    return lax.dot_general(a, b, dims, preferred_element_type=F32)


def _rms(x, w):
    ms = jnp.mean(x * x, axis=-1, keepdims=True)
    return x * lax.rsqrt(ms + RMS_EPS) * w


def _sigmoid(x):
    return 1.0 / (1.0 + jnp.exp(-x))


def _silu(x):
    return x * _sigmoid(x)


def _ffn_acc(x, nw_ref, wgu_ref, wd_ref, xn_ref, acc_ref):
    d_ff = wd_ref.shape[0]
    xn_ref[...] = _rms(x, nw_ref[...]).astype(BF16)
    for c in range(d_ff // FF_CHUNK):
        lo, hi = c * FF_CHUNK, (c + 1) * FF_CHUNK
        xn = xn_ref[...]
        g = _dot(xn, wgu_ref[:, lo:hi])
        u = _dot(xn, wgu_ref[:, d_ff + lo:d_ff + hi])
        part = _dot((_silu(g) * u).astype(BF16), wd_ref[lo:hi, :])
        if c == 0:
            acc_ref[...] = part
        else:
            acc_ref[...] += part


def _gla_proj(u, w_ref, wr_ref, wup_ref, bgk_ref, kd, vd):
    q = _dot(u, w_ref[:, 0:kd])
    k = _dot(u, w_ref[:, kd:2 * kd])
    v = _dot(u, w_ref[:, 2 * kd:2 * kd + vd])
    og = _dot(u, w_ref[:, 2 * kd + vd:2 * kd + 2 * vd])
    r = _dot(u, wr_ref[...]).astype(BF16)
    x = _dot(r, wup_ref[...]) + bgk_ref[...]
    g = (jnp.minimum(x, 0.0) - jnp.log1p(jnp.exp(-jnp.abs(x)))) * (1.0 / GLA_GATE_NORM)
    return q, k, v, g, og


def _hgrn_proj(u, w_ref, lbw_ref, fd, layer):
    lbw = lbw_ref[...]
    rows = [lbw[i:i + 1, :] for i in range(lbw.shape[0])]
    m = functools.reduce(jnp.maximum, rows)
    es = [jnp.exp(r - m) for r in rows]
    tot = functools.reduce(jnp.add, es)
    sm = [e / tot for e in es]
    lb = functools.reduce(jnp.add, sm[:layer + 1]) - sm[0]

    q = _silu(_dot(u, w_ref[:, 0:fd]))
    f = _dot(u, w_ref[:, fd:2 * fd])
    e = jnp.exp(-jnp.abs(f))
    inv = 1.0 / (1.0 + e)
    pos = f >= 0.0
    sig_f = jnp.where(pos, inv, e * inv)
    sig_nf = jnp.where(pos, e * inv, inv)
    g = jnp.log(lb + (1.0 - lb) * sig_f)
    k = (1.0 - lb) * sig_nf
    v = _dot(u, w_ref[:, 2 * fd:3 * fd])
    og = _dot(u, w_ref[:, 3 * fd:4 * fd])
    return q, k, v, g, og


def _pre_kernel(*refs, kind, hk, hv, layer):
    x_ref, fnw_ref, wgu_ref, wd_ref, nw_ref, w_ref = refs[:6]
    if kind == "gla":
        wr_ref, wup_ref, bgk_ref = refs[6:9]
        rest = refs[9:]
    else:
        lbw_ref = refs[6]
        rest = refs[7:]
    h_ref, q_ref, k_ref, v_ref, g_ref, og_ref, xn_ref, acc_ref = rest
    x = x_ref[...]
    _ffn_acc(x, fnw_ref, wgu_ref, wd_ref, xn_ref, acc_ref)
    h = x + 0.5 * acc_ref[...]
    h_ref[...] = h
    u = _rms(h, nw_ref[...]).astype(BF16)
    if kind == "gla":
        q, k, v, g, og = _gla_proj(u, w_ref, wr_ref, wup_ref, bgk_ref, hk, hv)
    else:
        q, k, v, g, og = _hgrn_proj(u, w_ref, lbw_ref, hk, layer)
    q_ref[...] = q
    k_ref[...] = k
    v_ref[...] = v.astype(BF16)
    g_ref[...] = g
    og_ref[...] = og


def _pre(x, kind, ffn, params, li, j, tm):
    n, d = x.shape
    fnw, wgu, wd = ffn
    if kind == "gla":
        nw, w, wr, wup, bgk = params
        hk, hv = wup.shape[2], (w.shape[2] - 2 * wup.shape[2]) // 2
        in_specs = [_layer(nw, li), _layer(w, j), _layer(wr, j), _layer(wup, j), _layer(bgk, j)]
    else:
        nw, w, lbw = params
        hk = hv = w.shape[2] // 4
        in_specs = [_layer(nw, li), _layer(w, j), _resident(lbw.shape)]
    outs = [((n, d), F32), ((n, hk), F32), ((n, hk), F32), ((n, hv), BF16), ((n, hk), F32), ((n, hv), F32)]
    return pl.pallas_call(
        functools.partial(_pre_kernel, kind=kind, hk=hk, hv=hv, layer=li),
        grid=(n // tm,),
        in_specs=[_rows(tm, d), _layer(fnw, li), _layer(wgu, li), _layer(wd, li)] + in_specs,
        out_specs=[_rows(tm, s[0][1]) for s in outs],
        out_shape=[jax.ShapeDtypeStruct(*s) for s in outs],
        scratch_shapes=[pltpu.VMEM((tm, d), BF16), pltpu.VMEM((tm, d), F32)],
        compiler_params=_cparams(1),
        name="pre_" + kind,
    )(x, fnw, wgu, wd, *params)


def _split3(x):
    x1 = x.astype(BF16)
    r1 = x - x1.astype(F32)
    x2 = r1.astype(BF16)
    x3 = (r1 - x2.astype(F32)).astype(BF16)
    return x1, x2, x3


def _rec_kernel(*refs, heads, kdim, vdim, chunk, sub, scale, single, t_real):
    refs = list(refs)
    q_ref, k_ref, v_ref, g_ref = refs[:4]
    if single:
        s0_ref, o_ref, s_ref, vb_s = refs[4:8]
        scratch = refs[8:]
    else:
        o_ref, s_ref, st_ref = refs[4:7]
        scratch = refs[7:]
    c, sb = chunk, sub
    n_sub = c // sb
    if n_sub > 1:
        qx_s, scratch = scratch[0], scratch[1:]
    qhat_s, kdec_s, kvar_s, dec_s, attn_s = scratch
    if n_sub == 1:
        qx_s = qhat_s
    n = pl.program_id(1)
    hk = heads * kdim
    cur = types.SimpleNamespace(rows=slice(0, c))

    if single:
        n_seq = q_ref.shape[0] // t_real
        per_tile = SUBLANES // t_real
        pad_rows = jnp.zeros((c - SUBLANES, 1), F32)
        v_rows = v_ref[...].astype(F32)

        def seq_rows(ref, s):
            tile, slot = divmod(s, per_tile)
            x = ref[tile * SUBLANES:(tile + 1) * SUBLANES, :].astype(F32)
            if slot:
                x = pltpu.roll(x, SUBLANES - slot * t_real, 0)
            keep = lax.broadcasted_iota(jnp.int32, x.shape, 0) < t_real
            return jnp.concatenate([jnp.where(keep, x, 0.0), jnp.broadcast_to(pad_rows, (c - SUBLANES, x.shape[1]))],
                                   axis=0)
    else:
        n_seq = q_ref.shape[0]
        seq_rows = lambda ref, s: ref[s, cur.rows]

        @pl.when(n == 0)
        def _init():
            st_ref[...] = jnp.zeros_like(st_ref)

    causal = (lax.broadcasted_iota(jnp.int32, (c, c), 0) >= lax.broadcasted_iota(jnp.int32, (c, c), 1))
    tri = causal.astype(F32).astype(BF16)
    tri3 = jnp.concatenate([tri, tri, tri], axis=1)

    def cumsum(s):
        return _dot(tri3, jnp.concatenate(_split3(seq_rows(g_ref, s)), axis=0))

    def prepare(s, b):
        q = seq_rows(q_ref, s) * scale
        k = seq_rows(k_ref, s)
        if single:
            vb_s[s] = seq_rows(v_rows, s).astype(BF16)
        blk_rows = [slice(i * sb, (i + 1) * sb) for i in range(n_sub)]
        eb = [jnp.exp(b[r] if i == 0 else b[r] - b[i * sb - 1:i * sb, :]) for i, r in enumerate(blk_rows)]
        blk_dec = [e[sb - 1:sb, :] for e in eb]

        def span(lo, hi):
            out = None
            for i in range(lo, hi):
                out = blk_dec[i] if out is None else out * blk_dec[i]
            return out

        mul = lambda x, d: x if d is None else x * d
        qx = [q[r] * eb[i] for i, r in enumerate(blk_rows)]
        kdiag = [k[r] * (1.0 / eb[i]) for i, r in enumerate(blk_rows)]
        kend = [kdiag[i] * blk_dec[i] for i in range(n_sub)]
        cat = lambda parts: parts[0] if len(parts) == 1 else jnp.concatenate(parts, axis=0)
        qhat_s[s] = cat([mul(qx[i], span(0, i)) for i in range(n_sub)]).astype(BF16)
        kdec_s[s] = cat([mul(kend[i], span(i + 1, n_sub)) for i in range(n_sub)]).astype(BF16)
        if n_sub > 1:
            qx_s[s] = cat(qx).astype(BF16)
        for i in range(n_sub):
            parts = [mul(kend[j], span(j + 1, i)) for j in range(i)] + [kdiag[i]]
            kv = cat(parts).astype(BF16)
            if i + 1 < n_sub:
                kv = jnp.concatenate([kv, jnp.zeros((c - (i + 1) * sb, hk), BF16)], axis=0)
            kvar_s[s, i] = kv
        if single:
            p1, p2, p3 = (p.astype(F32) for p in _split3(b[c - 1:c, :]))
            r = lax.broadcasted_iota(jnp.int32, (BF16_ROWS, hk), 0)
            dec_s[s] = jnp.where(r == 0, p1, jnp.where(r == 1, p2, jnp.where(r == 2, p3, 0.0))).astype(BF16)
        else:
            dec_s[s] = span(0, n_sub)

    def scores(s, h):
        ks = slice(h * kdim, (h + 1) * kdim)
        parts = [_dot(qx_s[s, blk * sb:(blk + 1) * sb, ks], kvar_s[s, blk, :, ks], _NT)
                 for blk in range(n_sub)]
        attn = parts[0] if n_sub == 1 else jnp.concatenate(parts, axis=0)
        attn_s[s, h] = jnp.where(causal, attn, 0.0).astype(BF16)

    def update(s, h):
        ks = slice(h * kdim, (h + 1) * kdim)
        vs = slice(h * vdim, (h + 1) * vdim)
        vh = vb_s[s, :, vs] if single else v_ref[s, cur.rows, vs]
        o_intra = _dot(attn_s[s, h], vh)
        if single:
            st = s0_ref[s, h]
            ones = jnp.ones((BF16_ROWS, LANES), BF16)
            dec = jnp.exp(_dot(dec_s[s, :, ks], ones, _TN))
            dec = jnp.concatenate([dec] * (vdim // LANES), axis=1)
            s_ref[s, h] = st * dec + _dot(kdec_s[s, :, ks], vh, _TN)
            return o_intra + _dot(qhat_s[s, :, ks], st.astype(BF16))
        else:
            st = st_ref[s, h]
            o_ref[s, cur.rows, vs] = o_intra + _dot(qhat_s[s, :, ks], st.astype(BF16), _NT)
            st_ref[s, h] = st * dec_s[s, :, ks] + _dot(vh, kdec_s[s, :, ks], _TN)

    pairs = [(s, h) for h in range(heads) for s in range(n_seq)]

    def stages():
        bs = [cumsum(s) for s in range(n_seq)]
        for s in range(n_seq):
            prepare(s, bs[s])
        for s, h in pairs:
            scores(s, h)

    if single:
        stages()
        row = lax.broadcasted_iota(jnp.int32, (SUBLANES, vdim), 0)
        for h in range(heads):
            for tile in range(n_seq // per_tile):
                merged = None
                for slot in range(per_tile):
                    o = jnp.where(row < t_real, update(tile * per_tile + slot, h)[:SUBLANES], 0.0)
                    if slot:
                        o = pltpu.roll(o, slot * t_real, 0)
                    merged = o if merged is None else merged + o
                o_ref[tile * SUBLANES:(tile + 1) * SUBLANES, h * vdim:(h + 1) * vdim] = merged
    else:
        for lo in range(0, q_ref.shape[1], c):
            cur.rows = slice(lo, lo + c)
            stages()
            for s, h in pairs:
                update(s, h)

        @pl.when(n == pl.num_programs(1) - 1)
        def _final():
            for s, h in pairs:
                s_ref[s, h] = st_ref[s, h].T


def _rec(q, k, v, g, s0, n_seq, heads, scale):
    hk, hv = q.shape[1], v.shape[1]
    t = q.shape[0] // n_seq
    kdim, vdim = hk // heads, hv // heads
    single = s0 is not None
    bb = min(REC_SEQS_SINGLE if single else REC_SEQS, n_seq)
    chunk = REC_SUB if single else REC_CHUNK
    sub = REC_SUB
    n_sub = chunk // sub
    state = pl.BlockSpec((bb, heads, kdim, vdim), lambda i, j: (i, 0, 0, 0))
    if single:
        assert SUBLANES % t == 0 and bb % (SUBLANES // t) == 0
        grid = (n_seq // bb, 1)
        tok = lambda w: pl.BlockSpec((bb * t, w), lambda i, j: (i, 0))
        args = [q, k, v, g, s0]
        in_specs = [tok(hk), tok(hk), tok(hv), tok(hk), state]
        o_shape = (n_seq * t, hv)
    else:
        step_rows = REC_CHUNKS_PER_STEP * chunk
        assert t % step_rows == 0
        grid = (n_seq // bb, t // step_rows)
        tok = lambda w: pl.BlockSpec((bb, step_rows, w), lambda i, j: (i, j, 0))
        args = [a.reshape(n_seq, t, a.shape[1]) for a in (q, k, v, g)]
        in_specs = [tok(hk), tok(hk), tok(hv), tok(hk)]
        o_shape = (n_seq, t, hv)
    scratch = []
    if single:
        scratch.append(pltpu.VMEM((bb, chunk, hv), BF16))
    else:
        scratch.append(pltpu.VMEM((bb, heads, vdim, kdim), F32))
    if n_sub > 1:
        scratch.append(pltpu.VMEM((bb, chunk, hk), BF16))
    scratch += [
        pltpu.VMEM((bb, chunk, hk), BF16),
        pltpu.VMEM((bb, chunk, hk), BF16),
        pltpu.VMEM((bb, n_sub, chunk, hk), BF16),
        pltpu.VMEM((bb, BF16_ROWS, hk), BF16) if single else pltpu.VMEM((bb, 1, hk), F32),
        pltpu.VMEM((bb, heads, chunk, chunk), BF16),
    ]
    o, s = pl.pallas_call(
        functools.partial(_rec_kernel, heads=heads, kdim=kdim, vdim=vdim, chunk=chunk, sub=sub,
                          scale=scale, single=single, t_real=t),
        grid=grid,
        in_specs=in_specs,
        out_specs=[tok(hv), state],
        out_shape=[jax.ShapeDtypeStruct(o_shape, F32),
                   jax.ShapeDtypeStruct((n_seq, heads, kdim, vdim), F32)],
        scratch_shapes=scratch,
        compiler_params=_cparams(2),
        name="rec",
    )(*args)
    return o.reshape(n_seq * t, hv), s


def _post_kernel(o_ref, og_ref, h_ref, p_ref, gn_ref, wo_ref, nw_ref, wgu_ref, wd_ref,
                 pn_ref, pwg_ref, pwp_ref, fn_ref, out_ref, xn_ref, acc_ref, *, heads, final):
    o = o_ref[...]
    vdim = o.shape[1] // heads
    gn = gn_ref[...]
    y = jnp.concatenate([_rms(o[:, h * vdim:(h + 1) * vdim], gn) for h in range(heads)], axis=1)
    y = (y * _silu(og_ref[...])).astype(BF16)
    out_ref[...] = h_ref[...] + _dot(y, wo_ref[...])
    _ffn_acc(out_ref[...], nw_ref, wgu_ref, wd_ref, xn_ref, acc_ref)
    h2 = out_ref[...] + 0.5 * acc_ref[...]
    gate = _sigmoid(_dot(_rms(h2, pn_ref[...]).astype(BF16), pwg_ref[...]))
    h3 = h2 + gate * _dot(p_ref[...].astype(BF16), pwp_ref[...])
    if final:
        h3 = _rms(h3, fn_ref[...])
    out_ref[...] = h3


def _post(o, og, h, p, mix_out, ffn, ple, li, j, heads, final, tm):
    n, hv = o.shape
    d = h.shape[1]
    gn, wo = mix_out
    nw, wgu, wd = ffn
    pn, pwg, pwp, fn = ple
    return pl.pallas_call(
        functools.partial(_post_kernel, heads=heads, final=final),
        grid=(n // tm,),
        in_specs=[_rows(tm, hv), _rows(tm, hv), _rows(tm, d),
                  pl.BlockSpec((None, tm, p.shape[2]), lambda i: (li, i, 0)),
                  _layer(gn, j), _layer(wo, j), _layer(nw, li), _layer(wgu, li), _layer(wd, li),
                  _layer(pn, li), _layer(pwg, li), _layer(pwp, li), _resident(fn.shape)],
        out_specs=_rows(tm, d),
        out_shape=jax.ShapeDtypeStruct((n, d), F32),
        scratch_shapes=[pltpu.VMEM((tm, d), BF16), pltpu.VMEM((tm, d), F32)],
        compiler_params=_cparams(1),
        name="post",
    )(o, og, h, p, gn, wo, nw, wgu, wd, pn, pwg, pwp, fn)


def _prep_weights(ffn1_norm, ffn1_w_gu, ffn1_w_down, mix_norm, gla_w_in, gla_w_gk_up, gla_b_gk, gla_gn,
                  gla_w_out, hgrn_w_in, hgrn_gn, hgrn_w_out, hgrn_lower_bounds, ffn2_norm, ffn2_w_gu,
                  ffn2_w_down, ple_norm, ple_w_gate, ple_w_proj, final_norm):
    row = lambda a: a.reshape(a.shape[0], 1, a.shape[1])
    rank = gla_w_gk_up.shape[1]
    main = gla_w_in.shape[2] - rank
    gla_w = gla_w_in.astype(BF16)
    return dict(
        ffn1=(row(ffn1_norm), ffn1_w_gu.astype(BF16), ffn1_w_down.astype(BF16)),
        ffn2=(row(ffn2_norm), ffn2_w_gu.astype(BF16), ffn2_w_down.astype(BF16)),
        mix_norm=row(mix_norm),
        gla_in=(gla_w[:, :, :main],
                jnp.pad(gla_w[:, :, main:], ((0, 0), (0, 0), (0, LANES - rank))),
                jnp.pad(gla_w_gk_up, ((0, 0), (0, LANES - rank), (0, 0))).astype(BF16),
                row(gla_b_gk)),
        gla_out=(row(gla_gn), gla_w_out.astype(BF16)),
        hgrn_in=(hgrn_w_in.astype(BF16), hgrn_lower_bounds),
        hgrn_out=(row(hgrn_gn), hgrn_w_out.astype(BF16)),
        ple=(row(ple_norm), ple_w_gate.astype(BF16), ple_w_proj.astype(BF16), final_norm.reshape(1, -1)),
    )


def _trunk(x, p, st_gla, st_hgrn, w, gla_heads, hgrn_heads):
    bsz, t, d = x.shape
    n = bsz * t
    depth = p.shape[0]
    tm = min(ROW_TILE, n)
    p = p.reshape(depth, n, p.shape[-1])
    h = x.reshape(n, d)
    new_gla, new_hgrn = [], []
    for li in range(depth):
        j = li // 2
        kind = "gla" if li % 2 == 0 else "hgrn"
        heads = gla_heads if kind == "gla" else hgrn_heads
        h, q, k, v, g, og = _pre(h, kind, w["ffn1"], (w["mix_norm"],) + w[kind + "_in"], li, j, tm)
        scale = (q.shape[1] // heads) ** -0.5 if kind == "gla" else 1.0
        s0 = st_gla if kind == "gla" else st_hgrn
        o, s = _rec(q, k, v, g, None if s0 is None else s0[j], bsz, heads, scale)
        (new_gla if kind == "gla" else new_hgrn).append(s)
        h = _post(o, og, h, p, w[kind + "_out"], w["ffn2"], w["ple"], li, j, heads, li == depth - 1, tm)
    return h.reshape(bsz, t, d), jnp.stack(new_gla), jnp.stack(new_hgrn)


def kernel(x_prompt, x_sample, state_gla, state_hgrn, p_prompt, p_sample, ffn1_norm, ffn1_w_gu, ffn1_w_down, mix_norm, gla_w_in, gla_w_gk_up, gla_b_gk, gla_gn, gla_w_out, hgrn_w_in, hgrn_gn, hgrn_w_out, hgrn_lower_bounds, ffn2_norm, ffn2_w_gu, ffn2_w_down, ple_norm, ple_w_gate, ple_w_proj, final_norm):
    w = _prep_weights(ffn1_norm, ffn1_w_gu, ffn1_w_down, mix_norm, gla_w_in, gla_w_gk_up, gla_b_gk, gla_gn,
                      gla_w_out, hgrn_w_in, hgrn_gn, hgrn_w_out, hgrn_lower_bounds, ffn2_norm, ffn2_w_gu,
                      ffn2_w_down, ple_norm, ple_w_gate, ple_w_proj, final_norm)
    gla_heads = state_gla.shape[2]
    hgrn_heads = state_hgrn.shape[2]
    y_p, gla_p, hgrn_p = _trunk(x_prompt, p_prompt, None, None, w, gla_heads, hgrn_heads)
    y_s, gla_s, hgrn_s = _trunk(x_sample, p_sample, state_gla, state_hgrn, w, gla_heads, hgrn_heads)
    return (y_p, y_s, gla_p, gla_s, hgrn_p, hgrn_s)
```

```python
import functools
import types

import jax
import jax.numpy as jnp
from jax import lax
from jax.experimental import pallas as pl
from jax.experimental.pallas import tpu as pltpu

F32 = jnp.float32
BF16 = jnp.bfloat16
RMS_EPS = 1e-6
GLA_GATE_NORM = 16.0
LANES = 128
BF16_ROWS = 16
FF_CHUNK = 256
ROW_TILE = 512
REC_CHUNK = 64
REC_SUB = 16
REC_CHUNKS_PER_STEP = 2
REC_SEQS = 8
REC_SEQS_SINGLE = 16
SUBLANES = 8
VMEM_LIMIT = 62 * 1024 * 1024

_NT = (((1,), (1,)), ((), ()))
_TN = (((0,), (0,)), ((), ()))


def _cparams(n_axes):
    return pltpu.CompilerParams(dimension_semantics=("arbitrary",) * n_axes,
                                vmem_limit_bytes=VMEM_LIMIT)


def _resident(shape):
    nd = len(shape)
    return pl.BlockSpec(shape, lambda *_: (0,) * nd, pipeline_mode=pl.Buffered(1))


def _layer(arr, li):
    nd = arr.ndim - 1
    return pl.BlockSpec((None,) + arr.shape[1:], lambda *_: (li,) + (0,) * nd,
                        pipeline_mode=pl.Buffered(1))


def _rows(tm, width):
    return pl.BlockSpec((tm, width), lambda i: (i, 0))


def _dot(a, b, dims=None):
    if dims is None:
        return jnp.dot(a, b, preferred_element_type=F32)
    return lax.dot_general(a, b, dims, preferred_element_type=F32)


def _rms(x, w):
    ms = jnp.mean(x * x, axis=-1, keepdims=True)
    return x * lax.rsqrt(ms + RMS_EPS) * w


def _sigmoid(x):
    return 1.0 / (1.0 + jnp.exp(-x))


def _silu(x):
    return x * _sigmoid(x)


def _ffn_acc(x, nw_ref, wgu_ref, wd_ref, xn_ref, act_ref):
    d_ff = wd_ref.shape[0]
    xn_ref[...] = _rms(x, nw_ref[...]).astype(BF16)
    for c in range(d_ff // FF_CHUNK):
        lo, hi = c * FF_CHUNK, (c + 1) * FF_CHUNK
        xn = xn_ref[...]
        g = _dot(xn, wgu_ref[:, lo:hi])
        u = _dot(xn, wgu_ref[:, d_ff + lo:d_ff + hi])
        act_ref[:, lo:hi] = (_silu(g) * u).astype(BF16)
    return _dot(act_ref[...], wd_ref[...])


def _gla_proj(u, w_ref, wr_ref, wup_ref, bgk_ref, kd, vd):
    q = _dot(u, w_ref[:, 0:kd])
    k = _dot(u, w_ref[:, kd:2 * kd])
    v = _dot(u, w_ref[:, 2 * kd:2 * kd + vd])
    og = _dot(u, w_ref[:, 2 * kd + vd:2 * kd + 2 * vd])
    r = _dot(u, wr_ref[...]).astype(BF16)
    x = _dot(r, wup_ref[...]) + bgk_ref[...]
    g = (jnp.minimum(x, 0.0) - jnp.log1p(jnp.exp(-jnp.abs(x)))) * (1.0 / GLA_GATE_NORM)
    return q, k, v, g, og


def _hgrn_proj(u, w_ref, lbw_ref, fd, layer):
    lbw = lbw_ref[...]
    rows = [lbw[i:i + 1, :] for i in range(lbw.shape[0])]
    m = functools.reduce(jnp.maximum, rows)
    es = [jnp.exp(r - m) for r in rows]
    tot = functools.reduce(jnp.add, es)
    sm = [e / tot for e in es]
    lb = functools.reduce(jnp.add, sm[:layer + 1]) - sm[0]

    q = _silu(_dot(u, w_ref[:, 0:fd]))
    f = _dot(u, w_ref[:, fd:2 * fd])
    e = jnp.exp(-jnp.abs(f))
    inv = 1.0 / (1.0 + e)
    pos = f >= 0.0
    sig_f = jnp.where(pos, inv, e * inv)
    sig_nf = jnp.where(pos, e * inv, inv)
    g = jnp.log(lb + (1.0 - lb) * sig_f)
    k = (1.0 - lb) * sig_nf
    v = _dot(u, w_ref[:, 2 * fd:3 * fd])
    og = _dot(u, w_ref[:, 3 * fd:4 * fd])
    return q, k, v, g, og


def _pre_kernel(*refs, kind, hk, hv, layer):
    x_ref, fnw_ref, wgu_ref, wd_ref, nw_ref, w_ref = refs[:6]
    if kind == "gla":
        wr_ref, wup_ref, bgk_ref = refs[6:9]
        rest = refs[9:]
    else:
        lbw_ref = refs[6]
        rest = refs[7:]
    h_ref, q_ref, k_ref, v_ref, g_ref, og_ref, xn_ref, acc_ref = rest
    x = x_ref[...]
    h = x + 0.5 * _ffn_acc(x, fnw_ref, wgu_ref, wd_ref, xn_ref, acc_ref)
    h_ref[...] = h
    u = _rms(h, nw_ref[...]).astype(BF16)
    if kind == "gla":
        q, k, v, g, og = _gla_proj(u, w_ref, wr_ref, wup_ref, bgk_ref, hk, hv)
    else:
        q, k, v, g, og = _hgrn_proj(u, w_ref, lbw_ref, hk, layer)
    q_ref[...] = q
    k_ref[...] = k
    v_ref[...] = v.astype(BF16)
    g_ref[...] = g
    og_ref[...] = og


def _pre(x, kind, ffn, params, li, j, tm):
    n, d = x.shape
    fnw, wgu, wd = ffn
    if kind == "gla":
        nw, w, wr, wup, bgk = params
        hk, hv = wup.shape[2], (w.shape[2] - 2 * wup.shape[2]) // 2
        in_specs = [_layer(nw, li), _layer(w, j), _layer(wr, j), _layer(wup, j), _layer(bgk, j)]
    else:
        nw, w, lbw = params
        hk = hv = w.shape[2] // 4
        in_specs = [_layer(nw, li), _layer(w, j), _resident(lbw.shape)]
    outs = [((n, d), F32), ((n, hk), F32), ((n, hk), F32), ((n, hv), BF16), ((n, hk), F32), ((n, hv), F32)]
    return pl.pallas_call(
        functools.partial(_pre_kernel, kind=kind, hk=hk, hv=hv, layer=li),
        grid=(n // tm,),
        in_specs=[_rows(tm, d), _layer(fnw, li), _layer(wgu, li), _layer(wd, li)] + in_specs,
        out_specs=[_rows(tm, s[0][1]) for s in outs],
        out_shape=[jax.ShapeDtypeStruct(*s) for s in outs],
        scratch_shapes=[pltpu.VMEM((tm, d), BF16), pltpu.VMEM((tm, wd.shape[1]), BF16)],
        compiler_params=_cparams(1),
        name="pre_" + kind,
    )(x, fnw, wgu, wd, *params)


def _split3(x):
    x1 = x.astype(BF16)
    r1 = x - x1.astype(F32)
    x2 = r1.astype(BF16)
    x3 = (r1 - x2.astype(F32)).astype(BF16)
    return x1, x2, x3


def _rec_kernel(*refs, heads, kdim, vdim, chunk, sub, scale, single, t_real):
    refs = list(refs)
    q_ref, k_ref, v_ref, g_ref = refs[:4]
    if single:
        s0_ref, o_ref, s_ref, vb_s = refs[4:8]
        scratch = refs[8:]
    else:
        o_ref, s_ref, st_ref = refs[4:7]
        scratch = refs[7:]
    c, sb = chunk, sub
    n_sub = c // sb
    if n_sub > 1:
        qx_s, scratch = scratch[0], scratch[1:]
    qhat_s, kdec_s, kvar_s, dec_s, attn_s = scratch
    if n_sub == 1:
        qx_s = qhat_s
    n = pl.program_id(1)
    hk = heads * kdim
    cur = types.SimpleNamespace(rows=slice(0, c))

    if single:
        n_seq = q_ref.shape[0] // t_real
        per_tile = SUBLANES // t_real
        pad_rows = jnp.zeros((c - SUBLANES, 1), F32)
        v_rows = v_ref[...].astype(F32)

        def seq_rows(ref, s):
            tile, slot = divmod(s, per_tile)
            x = ref[tile * SUBLANES:(tile + 1) * SUBLANES, :].astype(F32)
            if slot:
                x = pltpu.roll(x, SUBLANES - slot * t_real, 0)
            keep = lax.broadcasted_iota(jnp.int32, x.shape, 0) < t_real
            return jnp.concatenate([jnp.where(keep, x, 0.0), jnp.broadcast_to(pad_rows, (c - SUBLANES, x.shape[1]))],
                                   axis=0)
    else:
        n_seq = q_ref.shape[0]
        seq_rows = lambda ref, s: ref[s, cur.rows]

        @pl.when(n == 0)
        def _init():
            st_ref[...] = jnp.zeros_like(st_ref)

    causal = (lax.broadcasted_iota(jnp.int32, (c, c), 0) >= lax.broadcasted_iota(jnp.int32, (c, c), 1))
    tri = causal.astype(F32).astype(BF16)
    tri3 = jnp.concatenate([tri, tri, tri], axis=1)

    def cumsum(s):
        return _dot(tri3, jnp.concatenate(_split3(seq_rows(g_ref, s)), axis=0))

    def prepare(s, b):
        q = seq_rows(q_ref, s) * scale
        k = seq_rows(k_ref, s)
        if single:
            vb_s[s] = seq_rows(v_rows, s).astype(BF16)
        blk_rows = [slice(i * sb, (i + 1) * sb) for i in range(n_sub)]
        eb = [jnp.exp(b[r] if i == 0 else b[r] - b[i * sb - 1:i * sb, :]) for i, r in enumerate(blk_rows)]
        blk_dec = [e[sb - 1:sb, :] for e in eb]

        def span(lo, hi):
            out = None
            for i in range(lo, hi):
                out = blk_dec[i] if out is None else out * blk_dec[i]
            return out

        mul = lambda x, d: x if d is None else x * d
        qx = [q[r] * eb[i] for i, r in enumerate(blk_rows)]
        kdiag = [k[r] * (1.0 / eb[i]) for i, r in enumerate(blk_rows)]
        kend = [kdiag[i] * blk_dec[i] for i in range(n_sub)]
        cat = lambda parts: parts[0] if len(parts) == 1 else jnp.concatenate(parts, axis=0)
        qhat_s[s] = cat([mul(qx[i], span(0, i)) for i in range(n_sub)]).astype(BF16)
        kdec_s[s] = cat([mul(kend[i], span(i + 1, n_sub)) for i in range(n_sub)]).astype(BF16)
        if n_sub > 1:
            qx_s[s] = cat(qx).astype(BF16)
        for i in range(n_sub):
            parts = [mul(kend[j], span(j + 1, i)) for j in range(i)] + [kdiag[i]]
            kv = cat(parts).astype(BF16)
            if i + 1 < n_sub:
                kv = jnp.concatenate([kv, jnp.zeros((c - (i + 1) * sb, hk), BF16)], axis=0)
            kvar_s[s, i] = kv
        if single:
            p1, p2, p3 = (p.astype(F32) for p in _split3(b[c - 1:c, :]))
            r = lax.broadcasted_iota(jnp.int32, (BF16_ROWS, hk), 0)
            dec_s[s] = jnp.where(r == 0, p1, jnp.where(r == 1, p2, jnp.where(r == 2, p3, 0.0))).astype(BF16)
        else:
            dec_s[s] = span(0, n_sub)

    def scores(s, h):
        ks = slice(h * kdim, (h + 1) * kdim)
        parts = [_dot(qx_s[s, blk * sb:(blk + 1) * sb, ks], kvar_s[s, blk, :, ks], _NT)
                 for blk in range(n_sub)]
        attn = parts[0] if n_sub == 1 else jnp.concatenate(parts, axis=0)
        attn_s[s, h] = jnp.where(causal, attn, 0.0).astype(BF16)

    def update(s, h):
        ks = slice(h * kdim, (h + 1) * kdim)
        vs = slice(h * vdim, (h + 1) * vdim)
        vh = vb_s[s, :, vs] if single else v_ref[s, cur.rows, vs]
        o_intra = _dot(attn_s[s, h], vh)
        if single:
            st = s0_ref[s, h]
            ones = jnp.ones((BF16_ROWS, LANES), BF16)
            dec = jnp.exp(_dot(dec_s[s, :, ks], ones, _TN))
            dec = jnp.concatenate([dec] * (vdim // LANES), axis=1)
            s_ref[s, h] = st * dec + _dot(kdec_s[s, :, ks], vh, _TN)
            return o_intra + _dot(qhat_s[s, :, ks], st.astype(BF16))
        else:
            st = st_ref[s, h]
            o_ref[s, cur.rows, vs] = o_intra + _dot(qhat_s[s, :, ks], st.astype(BF16), _NT)
            st_ref[s, h] = st * dec_s[s, :, ks] + _dot(vh, kdec_s[s, :, ks], _TN)

    pairs = [(s, h) for h in range(heads) for s in range(n_seq)]

    def stages():
        bs = [cumsum(s) for s in range(n_seq)]
        for s in range(n_seq):
            prepare(s, bs[s])
        for s, h in pairs:
            scores(s, h)

    if single:
        stages()
        row = lax.broadcasted_iota(jnp.int32, (SUBLANES, vdim), 0)
        for h in range(heads):
            for tile in range(n_seq // per_tile):
                merged = None
                for slot in range(per_tile):
                    o = jnp.where(row < t_real, update(tile * per_tile + slot, h)[:SUBLANES], 0.0)
                    if slot:
                        o = pltpu.roll(o, slot * t_real, 0)
                    merged = o if merged is None else merged + o
                o_ref[tile * SUBLANES:(tile + 1) * SUBLANES, h * vdim:(h + 1) * vdim] = merged
    else:
        for lo in range(0, q_ref.shape[1], c):
            cur.rows = slice(lo, lo + c)
            stages()
            for s, h in pairs:
                update(s, h)

        @pl.when(n == pl.num_programs(1) - 1)
        def _final():
            for s, h in pairs:
                s_ref[s, h] = st_ref[s, h].T


def _rec(q, k, v, g, s0, n_seq, heads, scale):
    hk, hv = q.shape[1], v.shape[1]
    t = q.shape[0] // n_seq
    kdim, vdim = hk // heads, hv // heads
    single = s0 is not None
    bb = min(REC_SEQS_SINGLE if single else REC_SEQS, n_seq)
    chunk = REC_SUB if single else REC_CHUNK
    sub = REC_SUB
    n_sub = chunk // sub
    state = pl.BlockSpec((bb, heads, kdim, vdim), lambda i, j: (i, 0, 0, 0))
    if single:
        assert SUBLANES % t == 0 and bb % (SUBLANES // t) == 0
        grid = (n_seq // bb, 1)
        tok = lambda w: pl.BlockSpec((bb * t, w), lambda i, j: (i, 0))
        args = [q, k, v, g, s0]
        in_specs = [tok(hk), tok(hk), tok(hv), tok(hk), state]
        o_shape = (n_seq * t, hv)
    else:
        step_rows = REC_CHUNKS_PER_STEP * chunk
        assert t % step_rows == 0
        grid = (n_seq // bb, t // step_rows)
        tok = lambda w: pl.BlockSpec((bb, step_rows, w), lambda i, j: (i, j, 0))
        args = [a.reshape(n_seq, t, a.shape[1]) for a in (q, k, v, g)]
        in_specs = [tok(hk), tok(hk), tok(hv), tok(hk)]
        o_shape = (n_seq, t, hv)
    scratch = []
    if single:
        scratch.append(pltpu.VMEM((bb, chunk, hv), BF16))
    else:
        scratch.append(pltpu.VMEM((bb, heads, vdim, kdim), F32))
    if n_sub > 1:
        scratch.append(pltpu.VMEM((bb, chunk, hk), BF16))
    scratch += [
        pltpu.VMEM((bb, chunk, hk), BF16),
        pltpu.VMEM((bb, chunk, hk), BF16),
        pltpu.VMEM((bb, n_sub, chunk, hk), BF16),
        pltpu.VMEM((bb, BF16_ROWS, hk), BF16) if single else pltpu.VMEM((bb, 1, hk), F32),
        pltpu.VMEM((bb, heads, chunk, chunk), BF16),
    ]
    o, s = pl.pallas_call(
        functools.partial(_rec_kernel, heads=heads, kdim=kdim, vdim=vdim, chunk=chunk, sub=sub,
                          scale=scale, single=single, t_real=t),
        grid=grid,
        in_specs=in_specs,
        out_specs=[tok(hv), state],
        out_shape=[jax.ShapeDtypeStruct(o_shape, F32),
                   jax.ShapeDtypeStruct((n_seq, heads, kdim, vdim), F32)],
        scratch_shapes=scratch,
        compiler_params=_cparams(2),
        name="rec",
    )(*args)
    return o.reshape(n_seq * t, hv), s


def _post_kernel(o_ref, og_ref, h_ref, p_ref, gn_ref, wo_ref, nw_ref, wgu_ref, wd_ref,
                 pn_ref, pwg_ref, pwp_ref, fn_ref, out_ref, xn_ref, acc_ref, *, heads, final):
    o = o_ref[...]
    vdim = o.shape[1] // heads
    gn = gn_ref[...]
    y = jnp.concatenate([_rms(o[:, h * vdim:(h + 1) * vdim], gn) for h in range(heads)], axis=1)
    y = (y * _silu(og_ref[...])).astype(BF16)
    out_ref[...] = h_ref[...] + _dot(y, wo_ref[...])
    h2 = out_ref[...] + 0.5 * _ffn_acc(out_ref[...], nw_ref, wgu_ref, wd_ref, xn_ref, acc_ref)
    gate = _sigmoid(_dot(_rms(h2, pn_ref[...]).astype(BF16), pwg_ref[...]))
    h3 = h2 + gate * _dot(p_ref[...].astype(BF16), pwp_ref[...])
    if final:
        h3 = _rms(h3, fn_ref[...])
    out_ref[...] = h3


def _post(o, og, h, p, mix_out, ffn, ple, li, j, heads, final, tm):
    n, hv = o.shape
    d = h.shape[1]
    gn, wo = mix_out
    nw, wgu, wd = ffn
    pn, pwg, pwp, fn = ple
    return pl.pallas_call(
        functools.partial(_post_kernel, heads=heads, final=final),
        grid=(n // tm,),
        in_specs=[_rows(tm, hv), _rows(tm, hv), _rows(tm, d),
                  pl.BlockSpec((None, tm, p.shape[2]), lambda i: (li, i, 0)),
                  _layer(gn, j), _layer(wo, j), _layer(nw, li), _layer(wgu, li), _layer(wd, li),
                  _layer(pn, li), _layer(pwg, li), _layer(pwp, li), _resident(fn.shape)],
        out_specs=_rows(tm, d),
        out_shape=jax.ShapeDtypeStruct((n, d), F32),
        scratch_shapes=[pltpu.VMEM((tm, d), BF16), pltpu.VMEM((tm, wd.shape[1]), BF16)],
        compiler_params=_cparams(1),
        name="post",
    )(o, og, h, p, gn, wo, nw, wgu, wd, pn, pwg, pwp, fn)


def _prep_weights(ffn1_norm, ffn1_w_gu, ffn1_w_down, mix_norm, gla_w_in, gla_w_gk_up, gla_b_gk, gla_gn,
                  gla_w_out, hgrn_w_in, hgrn_gn, hgrn_w_out, hgrn_lower_bounds, ffn2_norm, ffn2_w_gu,
                  ffn2_w_down, ple_norm, ple_w_gate, ple_w_proj, final_norm):
    row = lambda a: a.reshape(a.shape[0], 1, a.shape[1])
    rank = gla_w_gk_up.shape[1]
    main = gla_w_in.shape[2] - rank
    gla_w = gla_w_in.astype(BF16)
    return dict(
        ffn1=(row(ffn1_norm), ffn1_w_gu.astype(BF16), ffn1_w_down.astype(BF16)),
        ffn2=(row(ffn2_norm), ffn2_w_gu.astype(BF16), ffn2_w_down.astype(BF16)),
        mix_norm=row(mix_norm),
        gla_in=(gla_w[:, :, :main],
                jnp.pad(gla_w[:, :, main:], ((0, 0), (0, 0), (0, LANES - rank))),
                jnp.pad(gla_w_gk_up, ((0, 0), (0, LANES - rank), (0, 0))).astype(BF16),
                row(gla_b_gk)),
        gla_out=(row(gla_gn), gla_w_out.astype(BF16)),
        hgrn_in=(hgrn_w_in.astype(BF16), hgrn_lower_bounds),
        hgrn_out=(row(hgrn_gn), hgrn_w_out.astype(BF16)),
        ple=(row(ple_norm), ple_w_gate.astype(BF16), ple_w_proj.astype(BF16), final_norm.reshape(1, -1)),
    )


def _trunk(x, p, st_gla, st_hgrn, w, gla_heads, hgrn_heads):
    bsz, t, d = x.shape
    n = bsz * t
    depth = p.shape[0]
    tm = min(ROW_TILE, n)
    p = p.reshape(depth, n, p.shape[-1])
    h = x.reshape(n, d)
    new_gla, new_hgrn = [], []
    for li in range(depth):
        j = li // 2
        kind = "gla" if li % 2 == 0 else "hgrn"
        heads = gla_heads if kind == "gla" else hgrn_heads
        h, q, k, v, g, og = _pre(h, kind, w["ffn1"], (w["mix_norm"],) + w[kind + "_in"], li, j, tm)
        scale = (q.shape[1] // heads) ** -0.5 if kind == "gla" else 1.0
        s0 = st_gla if kind == "gla" else st_hgrn
        o, s = _rec(q, k, v, g, None if s0 is None else s0[j], bsz, heads, scale)
        (new_gla if kind == "gla" else new_hgrn).append(s)
        h = _post(o, og, h, p, w[kind + "_out"], w["ffn2"], w["ple"], li, j, heads, li == depth - 1, tm)
    return h.reshape(bsz, t, d), jnp.stack(new_gla), jnp.stack(new_hgrn)


def kernel(x_prompt, x_sample, state_gla, state_hgrn, p_prompt, p_sample, ffn1_norm, ffn1_w_gu, ffn1_w_down, mix_norm, gla_w_in, gla_w_gk_up, gla_b_gk, gla_gn, gla_w_out, hgrn_w_in, hgrn_gn, hgrn_w_out, hgrn_lower_bounds, ffn2_norm, ffn2_w_gu, ffn2_w_down, ple_norm, ple_w_gate, ple_w_proj, final_norm):
    w = _prep_weights(ffn1_norm, ffn1_w_gu, ffn1_w_down, mix_norm, gla_w_in, gla_w_gk_up, gla_b_gk, gla_gn,
                      gla_w_out, hgrn_w_in, hgrn_gn, hgrn_w_out, hgrn_lower_bounds, ffn2_norm, ffn2_w_gu,
                      ffn2_w_down, ple_norm, ple_w_gate, ple_w_proj, final_norm)
    gla_heads = state_gla.shape[2]
    hgrn_heads = state_hgrn.shape[2]
    y_p, gla_p, hgrn_p = _trunk(x_prompt, p_prompt, None, None, w, gla_heads, hgrn_heads)
    y_s, gla_s, hgrn_s = _trunk(x_sample, p_sample, state_gla, state_hgrn, w, gla_heads, hgrn_heads)
    return (y_p, y_s, gla_p, gla_s, hgrn_p, hgrn_s)
```

```python
import functools
import types

import jax
import jax.numpy as jnp
from jax import lax
from jax.experimental import pallas as pl
from jax.experimental.pallas import tpu as pltpu

F32 = jnp.float32
BF16 = jnp.bfloat16
RMS_EPS = 1e-6
GLA_GATE_NORM = 16.0
LANES = 128
BF16_ROWS = 16
FF_CHUNK = 256
ROW_TILE = 512
REC_CHUNK = 64
REC_SUB = 16
REC_CHUNKS_PER_STEP = 2
REC_SEQS = 8
REC_SEQS_SINGLE = 16
SUBLANES = 8
VMEM_LIMIT = 62 * 1024 * 1024

_NT = (((1,), (1,)), ((), ()))
_TN = (((0,), (0,)), ((), ()))


def _cparams(n_axes):
    return pltpu.CompilerParams(dimension_semantics=("arbitrary",) * n_axes,
                                vmem_limit_bytes=VMEM_LIMIT)


def _resident(shape):
    nd = len(shape)
    return pl.BlockSpec(shape, lambda *_: (0,) * nd, pipeline_mode=pl.Buffered(1))


def _layer(arr, li):
    nd = arr.ndim - 1
    return pl.BlockSpec((None,) + arr.shape[1:], lambda *_: (li,) + (0,) * nd,
                        pipeline_mode=pl.Buffered(1))


def _rows(tm, width):
    return pl.BlockSpec((tm, width), lambda i: (i, 0))


def _dot(a, b, dims=None):
    if dims is None:
        return jnp.dot(a, b, preferred_element_type=F32)
    return lax.dot_general(a, b, dims, preferred_element_type=F32)


def _rms(x, w):
    ms = jnp.mean(x * x, axis=-1, keepdims=True)
    return x * lax.rsqrt(ms + RMS_EPS) * w


def _sigmoid(x):
    return 1.0 / (1.0 + jnp.exp(-x))


def _silu(x):
    return x * _sigmoid(x)


def _ffn_acc(x, nw_ref, wgu_ref, wd_ref, xn_ref, act_ref):
    d_ff = wd_ref.shape[0]
    xn_ref[...] = _rms(x, nw_ref[...]).astype(BF16)
    for c in range(d_ff // FF_CHUNK):
        lo, hi = c * FF_CHUNK, (c + 1) * FF_CHUNK
        xn = xn_ref[...]
        g = _dot(xn, wgu_ref[:, lo:hi])
        u = _dot(xn, wgu_ref[:, d_ff + lo:d_ff + hi])
        act_ref[:, lo:hi] = (_silu(g) * u).astype(BF16)
    return _dot(act_ref[...], wd_ref[...])


def _gla_proj(u, w_ref, wr_ref, wup_ref, bgk_ref, kd, vd):
    q = _dot(u, w_ref[:, 0:kd])
    k = _dot(u, w_ref[:, kd:2 * kd])
    v = _dot(u, w_ref[:, 2 * kd:2 * kd + vd])
    og = _dot(u, w_ref[:, 2 * kd + vd:2 * kd + 2 * vd])
    r = _dot(u, wr_ref[...]).astype(BF16)
    x = _dot(r, wup_ref[...]) + bgk_ref[...]
    g = (jnp.minimum(x, 0.0) - jnp.log1p(jnp.exp(-jnp.abs(x)))) * (1.0 / GLA_GATE_NORM)
    return q, k, v, g, og


def _hgrn_proj(u, w_ref, lbw_ref, fd, layer):
    lbw = lbw_ref[...]
    rows = [lbw[i:i + 1, :] for i in range(lbw.shape[0])]
    m = functools.reduce(jnp.maximum, rows)
    es = [jnp.exp(r - m) for r in rows]
    tot = functools.reduce(jnp.add, es)
    sm = [e / tot for e in es]
    lb = functools.reduce(jnp.add, sm[:layer + 1]) - sm[0]

    q = _silu(_dot(u, w_ref[:, 0:fd]))
    f = _dot(u, w_ref[:, fd:2 * fd])
    e = jnp.exp(-jnp.abs(f))
    inv = 1.0 / (1.0 + e)
    pos = f >= 0.0
    sig_f = jnp.where(pos, inv, e * inv)
    sig_nf = jnp.where(pos, e * inv, inv)
    g = jnp.log(lb + (1.0 - lb) * sig_f)
    k = (1.0 - lb) * sig_nf
    v = _dot(u, w_ref[:, 2 * fd:3 * fd])
    og = _dot(u, w_ref[:, 3 * fd:4 * fd])
    return q, k, v, g, og


def _pre_kernel(*refs, kind, hk, hv, layer):
    x_ref, fnw_ref, wgu_ref, wd_ref, nw_ref, w_ref = refs[:6]
    if kind == "gla":
        wr_ref, wup_ref, bgk_ref = refs[6:9]
        rest = refs[9:]
    else:
        lbw_ref = refs[6]
        rest = refs[7:]
    h_ref, q_ref, k_ref, v_ref, g_ref, og_ref, xn_ref, acc_ref = rest
    x = x_ref[...]
    h = x + 0.5 * _ffn_acc(x, fnw_ref, wgu_ref, wd_ref, xn_ref, acc_ref)
    h_ref[...] = h
    u = _rms(h, nw_ref[...]).astype(BF16)
    if kind == "gla":
        q, k, v, g, og = _gla_proj(u, w_ref, wr_ref, wup_ref, bgk_ref, hk, hv)
    else:
        q, k, v, g, og = _hgrn_proj(u, w_ref, lbw_ref, hk, layer)
    q_ref[...] = q
    k_ref[...] = k
    v_ref[...] = v.astype(BF16)
    g_ref[...] = g
    og_ref[...] = og


def _pre(x, kind, ffn, params, li, j, tm):
    n, d = x.shape
    fnw, wgu, wd = ffn
    if kind == "gla":
        nw, w, wr, wup, bgk = params
        hk, hv = wup.shape[2], (w.shape[2] - 2 * wup.shape[2]) // 2
        in_specs = [_layer(nw, li), _layer(w, j), _layer(wr, j), _layer(wup, j), _layer(bgk, j)]
    else:
        nw, w, lbw = params
        hk = hv = w.shape[2] // 4
        in_specs = [_layer(nw, li), _layer(w, j), _resident(lbw.shape)]
    outs = [((n, d), F32), ((n, hk), F32), ((n, hk), F32), ((n, hv), BF16), ((n, hk), F32), ((n, hv), F32)]
    return pl.pallas_call(
        functools.partial(_pre_kernel, kind=kind, hk=hk, hv=hv, layer=li),
        grid=(n // tm,),
        in_specs=[_rows(tm, d), _layer(fnw, li), _layer(wgu, li), _layer(wd, li)] + in_specs,
        out_specs=[_rows(tm, s[0][1]) for s in outs],
        out_shape=[jax.ShapeDtypeStruct(*s) for s in outs],
        scratch_shapes=[pltpu.VMEM((tm, d), BF16), pltpu.VMEM((tm, wd.shape[1]), BF16)],
        compiler_params=_cparams(1),
        name="pre_" + kind,
    )(x, fnw, wgu, wd, *params)


def _split3(x):
    x1 = x.astype(BF16)
    r1 = x - x1.astype(F32)
    x2 = r1.astype(BF16)
    x3 = (r1 - x2.astype(F32)).astype(BF16)
    return x1, x2, x3


def _rec_kernel(*refs, heads, kdim, vdim, chunk, sub, scale, single, t_real):
    refs = list(refs)
    q_ref, k_ref, v_ref, g_ref = refs[:4]
    if single:
        s0_ref, o_ref, s_ref, vb_s = refs[4:8]
        scratch = refs[8:]
    else:
        o_ref, s_ref = refs[4:6]
        scratch = refs[6:]
    c, sb = chunk, sub
    n_sub = c // sb
    if n_sub > 1:
        qx_s, scratch = scratch[0], scratch[1:]
    qhat_s, kdec_s, kvar_s, dec_s, attn_s = scratch
    if n_sub == 1:
        qx_s = qhat_s
    n = pl.program_id(1)
    hk = heads * kdim
    cur = types.SimpleNamespace(rows=slice(0, c))

    if single:
        n_seq = q_ref.shape[0] // t_real
        per_tile = SUBLANES // t_real
        pad_rows = jnp.zeros((c - SUBLANES, 1), F32)
        v_rows = v_ref[...].astype(F32)

        def seq_rows(ref, s):
            tile, slot = divmod(s, per_tile)
            x = ref[tile * SUBLANES:(tile + 1) * SUBLANES, :].astype(F32)
            if slot:
                x = pltpu.roll(x, SUBLANES - slot * t_real, 0)
            keep = lax.broadcasted_iota(jnp.int32, x.shape, 0) < t_real
            return jnp.concatenate([jnp.where(keep, x, 0.0), jnp.broadcast_to(pad_rows, (c - SUBLANES, x.shape[1]))],
                                   axis=0)
    else:
        n_seq = q_ref.shape[0]
        seq_rows = lambda ref, s: ref[s, cur.rows]

        @pl.when(n == 0)
        def _init():
            s_ref[...] = jnp.zeros_like(s_ref)

    causal = (lax.broadcasted_iota(jnp.int32, (c, c), 0) >= lax.broadcasted_iota(jnp.int32, (c, c), 1))
    tri = causal.astype(F32).astype(BF16)
    tri3 = jnp.concatenate([tri, tri, tri], axis=1)

    def cumsum(s):
        return _dot(tri3, jnp.concatenate(_split3(seq_rows(g_ref, s)), axis=0))

    def prepare(s, b):
        q = seq_rows(q_ref, s) * scale
        k = seq_rows(k_ref, s)
        if single:
            vb_s[s] = seq_rows(v_rows, s).astype(BF16)
        blk_rows = [slice(i * sb, (i + 1) * sb) for i in range(n_sub)]
        eb = [jnp.exp(b[r] if i == 0 else b[r] - b[i * sb - 1:i * sb, :]) for i, r in enumerate(blk_rows)]
        blk_dec = [e[sb - 1:sb, :] for e in eb]

        def span(lo, hi):
            out = None
            for i in range(lo, hi):
                out = blk_dec[i] if out is None else out * blk_dec[i]
            return out

        mul = lambda x, d: x if d is None else x * d
        qx = [q[r] * eb[i] for i, r in enumerate(blk_rows)]
        kdiag = [k[r] * (1.0 / eb[i]) for i, r in enumerate(blk_rows)]
        kend = [kdiag[i] * blk_dec[i] for i in range(n_sub)]
        cat = lambda parts: parts[0] if len(parts) == 1 else jnp.concatenate(parts, axis=0)
        qhat_s[s] = cat([mul(qx[i], span(0, i)) for i in range(n_sub)]).astype(BF16)
        kdec_s[s] = cat([mul(kend[i], span(i + 1, n_sub)) for i in range(n_sub)]).astype(BF16)
        if n_sub > 1:
            qx_s[s] = cat(qx).astype(BF16)
        for i in range(n_sub):
            parts = [mul(kend[j], span(j + 1, i)) for j in range(i)] + [kdiag[i]]
            kv = cat(parts).astype(BF16)
            if i + 1 < n_sub:
                kv = jnp.concatenate([kv, jnp.zeros((c - (i + 1) * sb, hk), BF16)], axis=0)
            kvar_s[s, i] = kv
        p1, p2, p3 = (p.astype(F32) for p in _split3(b[c - 1:c, :]))
        r = lax.broadcasted_iota(jnp.int32, (BF16_ROWS, hk), 0)
        dec_s[s] = jnp.where(r == 0, p1, jnp.where(r == 1, p2, jnp.where(r == 2, p3, 0.0))).astype(BF16)

    def scores(s, h):
        ks = slice(h * kdim, (h + 1) * kdim)
        parts = [_dot(qx_s[s, blk * sb:(blk + 1) * sb, ks], kvar_s[s, blk, :, ks], _NT)
                 for blk in range(n_sub)]
        attn = parts[0] if n_sub == 1 else jnp.concatenate(parts, axis=0)
        attn_s[s, h] = jnp.where(causal, attn, 0.0).astype(BF16)

    def update(s, h):
        ks = slice(h * kdim, (h + 1) * kdim)
        vs = slice(h * vdim, (h + 1) * vdim)
        vh = vb_s[s, :, vs] if single else v_ref[s, cur.rows, vs]
        st = (s0_ref if single else s_ref)[s, h]
        o = _dot(attn_s[s, h], vh) + _dot(qhat_s[s, :, ks], st.astype(BF16))
        ones = jnp.ones((BF16_ROWS, LANES), BF16)
        dec = jnp.exp(_dot(dec_s[s, :, ks], ones, _TN))
        dec = jnp.concatenate([dec] * (vdim // LANES), axis=1)
        s_ref[s, h] = st * dec + _dot(kdec_s[s, :, ks], vh, _TN)
        if single:
            return o
        o_ref[s, cur.rows, vs] = o

    pairs = [(s, h) for h in range(heads) for s in range(n_seq)]

    def stages():
        bs = [cumsum(s) for s in range(n_seq)]
        for s in range(n_seq):
            prepare(s, bs[s])
        for s, h in pairs:
            scores(s, h)

    if single:
        stages()
        row = lax.broadcasted_iota(jnp.int32, (SUBLANES, vdim), 0)
        for h in range(heads):
            for tile in range(n_seq // per_tile):
                merged = None
                for slot in range(per_tile):
                    o = jnp.where(row < t_real, update(tile * per_tile + slot, h)[:SUBLANES], 0.0)
                    if slot:
                        o = pltpu.roll(o, slot * t_real, 0)
                    merged = o if merged is None else merged + o
                o_ref[tile * SUBLANES:(tile + 1) * SUBLANES, h * vdim:(h + 1) * vdim] = merged
    else:
        for lo in range(0, q_ref.shape[1], c):
            cur.rows = slice(lo, lo + c)
            stages()
            for s, h in pairs:
                update(s, h)


def _rec(q, k, v, g, s0, n_seq, heads, scale):
    hk, hv = q.shape[1], v.shape[1]
    t = q.shape[0] // n_seq
    kdim, vdim = hk // heads, hv // heads
    single = s0 is not None
    bb = min(REC_SEQS_SINGLE if single else REC_SEQS, n_seq)
    chunk = REC_SUB if single else REC_CHUNK
    sub = REC_SUB
    n_sub = chunk // sub
    state = pl.BlockSpec((bb, heads, kdim, vdim), lambda i, j: (i, 0, 0, 0))
    if single:
        assert SUBLANES % t == 0 and bb % (SUBLANES // t) == 0
        grid = (n_seq // bb, 1)
        tok = lambda w: pl.BlockSpec((bb * t, w), lambda i, j: (i, 0))
        args = [q, k, v, g, s0]
        in_specs = [tok(hk), tok(hk), tok(hv), tok(hk), state]
        o_shape = (n_seq * t, hv)
    else:
        step_rows = REC_CHUNKS_PER_STEP * chunk
        assert t % step_rows == 0
        grid = (n_seq // bb, t // step_rows)
        tok = lambda w: pl.BlockSpec((bb, step_rows, w), lambda i, j: (i, j, 0))
        args = [a.reshape(n_seq, t, a.shape[1]) for a in (q, k, v, g)]
        in_specs = [tok(hk), tok(hk), tok(hv), tok(hk)]
        o_shape = (n_seq, t, hv)
    scratch = []
    if single:
        scratch.append(pltpu.VMEM((bb, chunk, hv), BF16))
    if n_sub > 1:
        scratch.append(pltpu.VMEM((bb, chunk, hk), BF16))
    scratch += [
        pltpu.VMEM((bb, chunk, hk), BF16),
        pltpu.VMEM((bb, chunk, hk), BF16),
        pltpu.VMEM((bb, n_sub, chunk, hk), BF16),
        pltpu.VMEM((bb, BF16_ROWS, hk), BF16),
        pltpu.VMEM((bb, heads, chunk, chunk), BF16),
    ]
    o, s = pl.pallas_call(
        functools.partial(_rec_kernel, heads=heads, kdim=kdim, vdim=vdim, chunk=chunk, sub=sub,
                          scale=scale, single=single, t_real=t),
        grid=grid,
        in_specs=in_specs,
        out_specs=[tok(hv), state],
        out_shape=[jax.ShapeDtypeStruct(o_shape, F32),
                   jax.ShapeDtypeStruct((n_seq, heads, kdim, vdim), F32)],
        scratch_shapes=scratch,
        compiler_params=_cparams(2),
        name="rec",
    )(*args)
    return o.reshape(n_seq * t, hv), s


def _post_kernel(o_ref, og_ref, h_ref, p_ref, gn_ref, wo_ref, nw_ref, wgu_ref, wd_ref,
                 pn_ref, pwg_ref, pwp_ref, fn_ref, out_ref, xn_ref, acc_ref, *, heads, final):
    o = o_ref[...]
    vdim = o.shape[1] // heads
    gn = gn_ref[...]
    y = jnp.concatenate([_rms(o[:, h * vdim:(h + 1) * vdim], gn) for h in range(heads)], axis=1)
    y = (y * _silu(og_ref[...])).astype(BF16)
    out_ref[...] = h_ref[...] + _dot(y, wo_ref[...])
    h2 = out_ref[...] + 0.5 * _ffn_acc(out_ref[...], nw_ref, wgu_ref, wd_ref, xn_ref, acc_ref)
    gate = _sigmoid(_dot(_rms(h2, pn_ref[...]).astype(BF16), pwg_ref[...]))
    h3 = h2 + gate * _dot(p_ref[...].astype(BF16), pwp_ref[...])
    if final:
        h3 = _rms(h3, fn_ref[...])
    out_ref[...] = h3


def _post(o, og, h, p, mix_out, ffn, ple, li, j, heads, final, tm):
    n, hv = o.shape
    d = h.shape[1]
    gn, wo = mix_out
    nw, wgu, wd = ffn
    pn, pwg, pwp, fn = ple
    return pl.pallas_call(
        functools.partial(_post_kernel, heads=heads, final=final),
        grid=(n // tm,),
        in_specs=[_rows(tm, hv), _rows(tm, hv), _rows(tm, d),
                  pl.BlockSpec((None, tm, p.shape[2]), lambda i: (li, i, 0)),
                  _layer(gn, j), _layer(wo, j), _layer(nw, li), _layer(wgu, li), _layer(wd, li),
                  _layer(pn, li), _layer(pwg, li), _layer(pwp, li), _resident(fn.shape)],
        out_specs=_rows(tm, d),
        out_shape=jax.ShapeDtypeStruct((n, d), F32),
        scratch_shapes=[pltpu.VMEM((tm, d), BF16), pltpu.VMEM((tm, wd.shape[1]), BF16)],
        compiler_params=_cparams(1),
        name="post",
    )(o, og, h, p, gn, wo, nw, wgu, wd, pn, pwg, pwp, fn)


def _prep_weights(ffn1_norm, ffn1_w_gu, ffn1_w_down, mix_norm, gla_w_in, gla_w_gk_up, gla_b_gk, gla_gn,
                  gla_w_out, hgrn_w_in, hgrn_gn, hgrn_w_out, hgrn_lower_bounds, ffn2_norm, ffn2_w_gu,
                  ffn2_w_down, ple_norm, ple_w_gate, ple_w_proj, final_norm):
    row = lambda a: a.reshape(a.shape[0], 1, a.shape[1])
    rank = gla_w_gk_up.shape[1]
    main = gla_w_in.shape[2] - rank
    gla_w = gla_w_in.astype(BF16)
    return dict(
        ffn1=(row(ffn1_norm), ffn1_w_gu.astype(BF16), ffn1_w_down.astype(BF16)),
        ffn2=(row(ffn2_norm), ffn2_w_gu.astype(BF16), ffn2_w_down.astype(BF16)),
        mix_norm=row(mix_norm),
        gla_in=(gla_w[:, :, :main],
                jnp.pad(gla_w[:, :, main:], ((0, 0), (0, 0), (0, LANES - rank))),
                jnp.pad(gla_w_gk_up, ((0, 0), (0, LANES - rank), (0, 0))).astype(BF16),
                row(gla_b_gk)),
        gla_out=(row(gla_gn), gla_w_out.astype(BF16)),
        hgrn_in=(hgrn_w_in.astype(BF16), hgrn_lower_bounds),
        hgrn_out=(row(hgrn_gn), hgrn_w_out.astype(BF16)),
        ple=(row(ple_norm), ple_w_gate.astype(BF16), ple_w_proj.astype(BF16), final_norm.reshape(1, -1)),
    )


def _trunk(x, p, st_gla, st_hgrn, w, gla_heads, hgrn_heads):
    bsz, t, d = x.shape
    n = bsz * t
    depth = p.shape[0]
    tm = min(ROW_TILE, n)
    p = p.reshape(depth, n, p.shape[-1])
    h = x.reshape(n, d)
    new_gla, new_hgrn = [], []
    for li in range(depth):
        j = li // 2
        kind = "gla" if li % 2 == 0 else "hgrn"
        heads = gla_heads if kind == "gla" else hgrn_heads
        h, q, k, v, g, og = _pre(h, kind, w["ffn1"], (w["mix_norm"],) + w[kind + "_in"], li, j, tm)
        scale = (q.shape[1] // heads) ** -0.5 if kind == "gla" else 1.0
        s0 = st_gla if kind == "gla" else st_hgrn
        o, s = _rec(q, k, v, g, None if s0 is None else s0[j], bsz, heads, scale)
        (new_gla if kind == "gla" else new_hgrn).append(s)
        h = _post(o, og, h, p, w[kind + "_out"], w["ffn2"], w["ple"], li, j, heads, li == depth - 1, tm)
    return h.reshape(bsz, t, d), jnp.stack(new_gla), jnp.stack(new_hgrn)


def kernel(x_prompt, x_sample, state_gla, state_hgrn, p_prompt, p_sample, ffn1_norm, ffn1_w_gu, ffn1_w_down, mix_norm, gla_w_in, gla_w_gk_up, gla_b_gk, gla_gn, gla_w_out, hgrn_w_in, hgrn_gn, hgrn_w_out, hgrn_lower_bounds, ffn2_norm, ffn2_w_gu, ffn2_w_down, ple_norm, ple_w_gate, ple_w_proj, final_norm):
    w = _prep_weights(ffn1_norm, ffn1_w_gu, ffn1_w_down, mix_norm, gla_w_in, gla_w_gk_up, gla_b_gk, gla_gn,
                      gla_w_out, hgrn_w_in, hgrn_gn, hgrn_w_out, hgrn_lower_bounds, ffn2_norm, ffn2_w_gu,
                      ffn2_w_down, ple_norm, ple_w_gate, ple_w_proj, final_norm)
    gla_heads = state_gla.shape[2]
    hgrn_heads = state_hgrn.shape[2]
    y_p, gla_p, hgrn_p = _trunk(x_prompt, p_prompt, None, None, w, gla_heads, hgrn_heads)
    y_s, gla_s, hgrn_s = _trunk(x_sample, p_sample, state_gla, state_hgrn, w, gla_heads, hgrn_heads)
    return (y_p, y_s, gla_p, gla_s, hgrn_p, hgrn_s)
```

```python
import functools
import types

import jax
import jax.numpy as jnp
from jax import lax
from jax.experimental import pallas as pl
from jax.experimental.pallas import tpu as pltpu

F32 = jnp.float32
BF16 = jnp.bfloat16
RMS_EPS = 1e-6
GLA_GATE_NORM = 16.0
LANES = 128
BF16_ROWS = 16
FF_CHUNK = 256
ROW_TILE = 512
REC_CHUNK = 64
REC_SUB = 16
REC_CHUNKS_PER_STEP = 2
REC_SEQS = 8
REC_SEQS_SINGLE = 16
SUBLANES = 8
VMEM_LIMIT = 62 * 1024 * 1024

_NT = (((1,), (1,)), ((), ()))
_TN = (((0,), (0,)), ((), ()))


def _cparams(n_axes):
    return pltpu.CompilerParams(dimension_semantics=("arbitrary",) * n_axes,
                                vmem_limit_bytes=VMEM_LIMIT)


def _resident(shape):
    nd = len(shape)
    return pl.BlockSpec(shape, lambda *_: (0,) * nd, pipeline_mode=pl.Buffered(1))


def _layer(arr, li):
    nd = arr.ndim - 1
    return pl.BlockSpec((None,) + arr.shape[1:], lambda *_: (li,) + (0,) * nd,
                        pipeline_mode=pl.Buffered(1))


def _rows(tm, width):
    return pl.BlockSpec((tm, width), lambda i: (i, 0))


def _dot(a, b, dims=None):
    if dims is None:
        return jnp.dot(a, b, preferred_element_type=F32)
    return lax.dot_general(a, b, dims, preferred_element_type=F32)


def _rms(x, w):
    ms = jnp.mean(x * x, axis=-1, keepdims=True)
    return x * lax.rsqrt(ms + RMS_EPS) * w


def _sigmoid(x):
    return 1.0 / (1.0 + jnp.exp(-x))


def _silu(x):
    return x * _sigmoid(x)


def _ffn_acc(x, nw_ref, wgu_ref, wd_ref, xn_ref, act_ref):
    d_ff = wd_ref.shape[0]
    xn_ref[...] = _rms(x, nw_ref[...]).astype(BF16)
    for c in range(d_ff // FF_CHUNK):
        lo, hi = c * FF_CHUNK, (c + 1) * FF_CHUNK
        xn = xn_ref[...]
        g = _dot(xn, wgu_ref[:, lo:hi])
        u = _dot(xn, wgu_ref[:, d_ff + lo:d_ff + hi])
        act_ref[:, lo:hi] = (_silu(g) * u).astype(BF16)
    return _dot(act_ref[...], wd_ref[...])


def _gla_proj(u, w_ref, wr_ref, wup_ref, bgk_ref, kd, vd):
    q = _dot(u, w_ref[:, 0:kd])
    k = _dot(u, w_ref[:, kd:2 * kd])
    v = _dot(u, w_ref[:, 2 * kd:2 * kd + vd])
    og = _dot(u, w_ref[:, 2 * kd + vd:2 * kd + 2 * vd])
    r = _dot(u, wr_ref[...]).astype(BF16)
    x = _dot(r, wup_ref[...]) + bgk_ref[...]
    g = (jnp.minimum(x, 0.0) - jnp.log1p(jnp.exp(-jnp.abs(x)))) * (1.0 / GLA_GATE_NORM)
    return q, k, v, g, og


def _hgrn_proj(u, w_ref, lbw_ref, fd, layer):
    lbw = lbw_ref[...]
    rows = [lbw[i:i + 1, :] for i in range(lbw.shape[0])]
    m = functools.reduce(jnp.maximum, rows)
    es = [jnp.exp(r - m) for r in rows]
    tot = functools.reduce(jnp.add, es)
    sm = [e / tot for e in es]
    lb = functools.reduce(jnp.add, sm[:layer + 1]) - sm[0]

    q = _silu(_dot(u, w_ref[:, 0:fd]))
    f = _dot(u, w_ref[:, fd:2 * fd])
    e = jnp.exp(-jnp.abs(f))
    inv = 1.0 / (1.0 + e)
    pos = f >= 0.0
    sig_f = jnp.where(pos, inv, e * inv)
    sig_nf = jnp.where(pos, e * inv, inv)
    g = jnp.log(lb + (1.0 - lb) * sig_f)
    k = (1.0 - lb) * sig_nf
    v = _dot(u, w_ref[:, 2 * fd:3 * fd])
    og = _dot(u, w_ref[:, 3 * fd:4 * fd])
    return q, k, v, g, og


def _pre_kernel(*refs, kind, hk, hv, layer):
    x_ref, fnw_ref, wgu_ref, wd_ref, nw_ref, w_ref = refs[:6]
    if kind == "gla":
        wr_ref, wup_ref, bgk_ref = refs[6:9]
        rest = refs[9:]
    else:
        lbw_ref = refs[6]
        rest = refs[7:]
    h_ref, qkg_ref, v_ref, og_ref, xn_ref, acc_ref = rest
    x = x_ref[...]
    h = x + 0.5 * _ffn_acc(x, fnw_ref, wgu_ref, wd_ref, xn_ref, acc_ref)
    h_ref[...] = h
    u = _rms(h, nw_ref[...]).astype(BF16)
    if kind == "gla":
        q, k, v, g, og = _gla_proj(u, w_ref, wr_ref, wup_ref, bgk_ref, hk, hv)
    else:
        q, k, v, g, og = _hgrn_proj(u, w_ref, lbw_ref, hk, layer)
    qkg_ref[:, 0:hk] = q
    qkg_ref[:, hk:2 * hk] = k
    qkg_ref[:, 2 * hk:3 * hk] = g
    v_ref[...] = v.astype(BF16)
    og_ref[...] = og


def _pre(x, kind, ffn, params, li, j, tm):
    n, d = x.shape
    fnw, wgu, wd = ffn
    if kind == "gla":
        nw, w, wr, wup, bgk = params
        hk, hv = wup.shape[2], (w.shape[2] - 2 * wup.shape[2]) // 2
        in_specs = [_layer(nw, li), _layer(w, j), _layer(wr, j), _layer(wup, j), _layer(bgk, j)]
    else:
        nw, w, lbw = params
        hk = hv = w.shape[2] // 4
        in_specs = [_layer(nw, li), _layer(w, j), _resident(lbw.shape)]
    outs = [((n, d), F32), ((n, 3 * hk), F32), ((n, hv), BF16), ((n, hv), F32)]
    return pl.pallas_call(
        functools.partial(_pre_kernel, kind=kind, hk=hk, hv=hv, layer=li),
        grid=(n // tm,),
        in_specs=[_rows(tm, d), _layer(fnw, li), _layer(wgu, li), _layer(wd, li)] + in_specs,
        out_specs=[_rows(tm, s[0][1]) for s in outs],
        out_shape=[jax.ShapeDtypeStruct(*s) for s in outs],
        scratch_shapes=[pltpu.VMEM((tm, d), BF16), pltpu.VMEM((tm, wd.shape[1]), BF16)],
        compiler_params=_cparams(1),
        name="pre_" + kind,
    )(x, fnw, wgu, wd, *params)


def _split3(x):
    x1 = x.astype(BF16)
    r1 = x - x1.astype(F32)
    x2 = r1.astype(BF16)
    x3 = (r1 - x2.astype(F32)).astype(BF16)
    return x1, x2, x3


def _rec_kernel(*refs, heads, kdim, vdim, chunk, sub, scale, single, t_real):
    refs = list(refs)
    qkg_ref, v_ref = refs[:2]
    if single:
        s0_ref, o_ref, s_ref, vb_s = refs[2:6]
        scratch = refs[6:]
    else:
        o_ref, s_ref, st_ref = refs[2:5]
        scratch = refs[5:]
    c, sb = chunk, sub
    n_sub = c // sb
    if n_sub > 1:
        qx_s, scratch = scratch[0], scratch[1:]
    qhat_s, kdec_s, kvar_s, dec_s, attn_s = scratch
    if n_sub == 1:
        qx_s = qhat_s
    n = pl.program_id(1)
    hk = heads * kdim
    cur = types.SimpleNamespace(rows=slice(0, c))

    if single:
        n_seq = qkg_ref.shape[0] // t_real
        per_tile = SUBLANES // t_real
        pad_rows = jnp.zeros((c - SUBLANES, 1), F32)
        v_rows = v_ref[...].astype(F32)

        def seq_rows(ref, s):
            tile, slot = divmod(s, per_tile)
            x = ref[tile * SUBLANES:(tile + 1) * SUBLANES, :].astype(F32)
            if slot:
                x = pltpu.roll(x, SUBLANES - slot * t_real, 0)
            keep = lax.broadcasted_iota(jnp.int32, x.shape, 0) < t_real
            return jnp.concatenate([jnp.where(keep, x, 0.0), jnp.broadcast_to(pad_rows, (c - SUBLANES, x.shape[1]))],
                                   axis=0)
    else:
        n_seq = qkg_ref.shape[0]
        seq_rows = lambda ref, s: ref[s, cur.rows]

        @pl.when(n == 0)
        def _init():
            st_ref[...] = jnp.zeros_like(st_ref)

    causal = (lax.broadcasted_iota(jnp.int32, (c, c), 0) >= lax.broadcasted_iota(jnp.int32, (c, c), 1))
    tri = causal.astype(F32).astype(BF16)
    tri3 = jnp.concatenate([tri, tri, tri], axis=1)

    def cumsum(s):
        return _dot(tri3, jnp.concatenate(_split3(seq_rows(qkg_ref, s)[:, 2 * hk:]), axis=0))

    def prepare(s, b):
        qk = seq_rows(qkg_ref, s)
        q = qk[:, 0:hk] * scale
        k = qk[:, hk:2 * hk]
        if single:
            vb_s[s] = seq_rows(v_rows, s).astype(BF16)
        blk_rows = [slice(i * sb, (i + 1) * sb) for i in range(n_sub)]
        eb = [jnp.exp(b[r] if i == 0 else b[r] - b[i * sb - 1:i * sb, :]) for i, r in enumerate(blk_rows)]
        blk_dec = [e[sb - 1:sb, :] for e in eb]

        def span(lo, hi):
            out = None
            for i in range(lo, hi):
                out = blk_dec[i] if out is None else out * blk_dec[i]
            return out

        mul = lambda x, d: x if d is None else x * d
        qx = [q[r] * eb[i] for i, r in enumerate(blk_rows)]
        kdiag = [k[r] * (1.0 / eb[i]) for i, r in enumerate(blk_rows)]
        kend = [kdiag[i] * blk_dec[i] for i in range(n_sub)]
        cat = lambda parts: parts[0] if len(parts) == 1 else jnp.concatenate(parts, axis=0)
        qhat_s[s] = cat([mul(qx[i], span(0, i)) for i in range(n_sub)]).astype(BF16)
        kdec_s[s] = cat([mul(kend[i], span(i + 1, n_sub)) for i in range(n_sub)]).astype(BF16)
        if n_sub > 1:
            qx_s[s] = cat(qx).astype(BF16)
        for i in range(n_sub):
            parts = [mul(kend[j], span(j + 1, i)) for j in range(i)] + [kdiag[i]]
            kv = cat(parts).astype(BF16)
            if i + 1 < n_sub:
                kv = jnp.concatenate([kv, jnp.zeros((c - (i + 1) * sb, hk), BF16)], axis=0)
            kvar_s[s, i] = kv
        if single:
            p1, p2, p3 = (p.astype(F32) for p in _split3(b[c - 1:c, :]))
            r = lax.broadcasted_iota(jnp.int32, (BF16_ROWS, hk), 0)
            dec_s[s] = jnp.where(r == 0, p1, jnp.where(r == 1, p2, jnp.where(r == 2, p3, 0.0))).astype(BF16)
        else:
            dec_s[s] = span(0, n_sub)

    def scores(s, h):
        ks = slice(h * kdim, (h + 1) * kdim)
        parts = [_dot(qx_s[s, blk * sb:(blk + 1) * sb, ks], kvar_s[s, blk, :, ks], _NT)
                 for blk in range(n_sub)]
        attn = parts[0] if n_sub == 1 else jnp.concatenate(parts, axis=0)
        attn_s[s, h] = jnp.where(causal, attn, 0.0).astype(BF16)

    def update(s, h):
        ks = slice(h * kdim, (h + 1) * kdim)
        vs = slice(h * vdim, (h + 1) * vdim)
        vh = vb_s[s, :, vs] if single else v_ref[s, cur.rows, vs]
        o_intra = _dot(attn_s[s, h], vh)
        if single:
            st = s0_ref[s, h]
            ones = jnp.ones((BF16_ROWS, LANES), BF16)
            dec = jnp.exp(_dot(dec_s[s, :, ks], ones, _TN))
            dec = jnp.concatenate([dec] * (vdim // LANES), axis=1)
            s_ref[s, h] = st * dec + _dot(kdec_s[s, :, ks], vh, _TN)
            return o_intra + _dot(qhat_s[s, :, ks], st.astype(BF16))
        else:
            st = st_ref[s, h]
            o_ref[s, cur.rows, vs] = o_intra + _dot(qhat_s[s, :, ks], st.astype(BF16), _NT)
            st_ref[s, h] = st * dec_s[s, :, ks] + _dot(vh, kdec_s[s, :, ks], _TN)

    pairs = [(s, h) for h in range(heads) for s in range(n_seq)]

    def stages():
        bs = [cumsum(s) for s in range(n_seq)]
        for s in range(n_seq):
            prepare(s, bs[s])
        for s, h in pairs:
            scores(s, h)

    if single:
        stages()
        row = lax.broadcasted_iota(jnp.int32, (SUBLANES, vdim), 0)
        for h in range(heads):
            for tile in range(n_seq // per_tile):
                merged = None
                for slot in range(per_tile):
                    o = jnp.where(row < t_real, update(tile * per_tile + slot, h)[:SUBLANES], 0.0)
                    if slot:
                        o = pltpu.roll(o, slot * t_real, 0)
                    merged = o if merged is None else merged + o
                o_ref[tile * SUBLANES:(tile + 1) * SUBLANES, h * vdim:(h + 1) * vdim] = merged
    else:
        for lo in range(0, qkg_ref.shape[1], c):
            cur.rows = slice(lo, lo + c)
            stages()
            for s, h in pairs:
                update(s, h)

        @pl.when(n == pl.num_programs(1) - 1)
        def _final():
            for s, h in pairs:
                s_ref[s, h] = st_ref[s, h].T


def _rec(qkg, v, s0, n_seq, heads, scale):
    hk, hv = qkg.shape[1] // 3, v.shape[1]
    t = v.shape[0] // n_seq
    kdim, vdim = hk // heads, hv // heads
    single = s0 is not None
    bb = min(REC_SEQS_SINGLE if single else REC_SEQS, n_seq)
    chunk = REC_SUB if single else REC_CHUNK
    sub = REC_SUB
    n_sub = chunk // sub
    state = pl.BlockSpec((bb, heads, kdim, vdim), lambda i, j: (i, 0, 0, 0))
    if single:
        assert SUBLANES % t == 0 and bb % (SUBLANES // t) == 0
        grid = (n_seq // bb, 1)
        tok = lambda w: pl.BlockSpec((bb * t, w), lambda i, j: (i, 0))
        args = [qkg, v, s0]
        in_specs = [tok(3 * hk), tok(hv), state]
        o_shape = (n_seq * t, hv)
    else:
        step_rows = REC_CHUNKS_PER_STEP * chunk
        assert t % step_rows == 0
        grid = (n_seq // bb, t // step_rows)
        tok = lambda w: pl.BlockSpec((bb, step_rows, w), lambda i, j: (i, j, 0))
        args = [a.reshape(n_seq, t, a.shape[1]) for a in (qkg, v)]
        in_specs = [tok(3 * hk), tok(hv)]
        o_shape = (n_seq, t, hv)
    scratch = []
    if single:
        scratch.append(pltpu.VMEM((bb, chunk, hv), BF16))
    else:
        scratch.append(pltpu.VMEM((bb, heads, vdim, kdim), F32))
    if n_sub > 1:
        scratch.append(pltpu.VMEM((bb, chunk, hk), BF16))
    scratch += [
        pltpu.VMEM((bb, chunk, hk), BF16),
        pltpu.VMEM((bb, chunk, hk), BF16),
        pltpu.VMEM((bb, n_sub, chunk, hk), BF16),
        pltpu.VMEM((bb, BF16_ROWS, hk), BF16) if single else pltpu.VMEM((bb, 1, hk), F32),
        pltpu.VMEM((bb, heads, chunk, chunk), BF16),
    ]
    o, s = pl.pallas_call(
        functools.partial(_rec_kernel, heads=heads, kdim=kdim, vdim=vdim, chunk=chunk, sub=sub,
                          scale=scale, single=single, t_real=t),
        grid=grid,
        in_specs=in_specs,
        out_specs=[tok(hv), state],
        out_shape=[jax.ShapeDtypeStruct(o_shape, F32),
                   jax.ShapeDtypeStruct((n_seq, heads, kdim, vdim), F32)],
        scratch_shapes=scratch,
        compiler_params=_cparams(2),
        name="rec",
    )(*args)
    return o.reshape(n_seq * t, hv), s


def _post_kernel(o_ref, og_ref, h_ref, p_ref, gn_ref, wo_ref, nw_ref, wgu_ref, wd_ref,
                 pn_ref, pwg_ref, pwp_ref, fn_ref, out_ref, xn_ref, acc_ref, *, heads, final):
    o = o_ref[...]
    vdim = o.shape[1] // heads
    gn = gn_ref[...]
    y = jnp.concatenate([_rms(o[:, h * vdim:(h + 1) * vdim], gn) for h in range(heads)], axis=1)
    y = (y * _silu(og_ref[...])).astype(BF16)
    out_ref[...] = h_ref[...] + _dot(y, wo_ref[...])
    h2 = out_ref[...] + 0.5 * _ffn_acc(out_ref[...], nw_ref, wgu_ref, wd_ref, xn_ref, acc_ref)
    gate = _sigmoid(_dot(_rms(h2, pn_ref[...]).astype(BF16), pwg_ref[...]))
    h3 = h2 + gate * _dot(p_ref[...].astype(BF16), pwp_ref[...])
    if final:
        h3 = _rms(h3, fn_ref[...])
    out_ref[...] = h3


def _post(o, og, h, p, mix_out, ffn, ple, li, j, heads, final, tm):
    n, hv = o.shape
    d = h.shape[1]
    gn, wo = mix_out
    nw, wgu, wd = ffn
    pn, pwg, pwp, fn = ple
    return pl.pallas_call(
        functools.partial(_post_kernel, heads=heads, final=final),
        grid=(n // tm,),
        in_specs=[_rows(tm, hv), _rows(tm, hv), _rows(tm, d),
                  pl.BlockSpec((None, tm, p.shape[2]), lambda i: (li, i, 0)),
                  _layer(gn, j), _layer(wo, j), _layer(nw, li), _layer(wgu, li), _layer(wd, li),
                  _layer(pn, li), _layer(pwg, li), _layer(pwp, li), _resident(fn.shape)],
        out_specs=_rows(tm, d),
        out_shape=jax.ShapeDtypeStruct((n, d), F32),
        scratch_shapes=[pltpu.VMEM((tm, d), BF16), pltpu.VMEM((tm, wd.shape[1]), BF16)],
        compiler_params=_cparams(1),
        name="post",
    )(o, og, h, p, gn, wo, nw, wgu, wd, pn, pwg, pwp, fn)


def _prep_weights(ffn1_norm, ffn1_w_gu, ffn1_w_down, mix_norm, gla_w_in, gla_w_gk_up, gla_b_gk, gla_gn,
                  gla_w_out, hgrn_w_in, hgrn_gn, hgrn_w_out, hgrn_lower_bounds, ffn2_norm, ffn2_w_gu,
                  ffn2_w_down, ple_norm, ple_w_gate, ple_w_proj, final_norm):
    row = lambda a: a.reshape(a.shape[0], 1, a.shape[1])
    rank = gla_w_gk_up.shape[1]
    main = gla_w_in.shape[2] - rank
    gla_w = gla_w_in.astype(BF16)
    return dict(
        ffn1=(row(ffn1_norm), ffn1_w_gu.astype(BF16), ffn1_w_down.astype(BF16)),
        ffn2=(row(ffn2_norm), ffn2_w_gu.astype(BF16), ffn2_w_down.astype(BF16)),
        mix_norm=row(mix_norm),
        gla_in=(gla_w[:, :, :main],
                jnp.pad(gla_w[:, :, main:], ((0, 0), (0, 0), (0, LANES - rank))),
                jnp.pad(gla_w_gk_up, ((0, 0), (0, LANES - rank), (0, 0))).astype(BF16),
                row(gla_b_gk)),
        gla_out=(row(gla_gn), gla_w_out.astype(BF16)),
        hgrn_in=(hgrn_w_in.astype(BF16), hgrn_lower_bounds),
        hgrn_out=(row(hgrn_gn), hgrn_w_out.astype(BF16)),
        ple=(row(ple_norm), ple_w_gate.astype(BF16), ple_w_proj.astype(BF16), final_norm.reshape(1, -1)),
    )


def _trunk(x, p, st_gla, st_hgrn, w, gla_heads, hgrn_heads):
    bsz, t, d = x.shape
    n = bsz * t
    depth = p.shape[0]
    tm = min(ROW_TILE, n)
    p = p.reshape(depth, n, p.shape[-1])
    h = x.reshape(n, d)
    new_gla, new_hgrn = [], []
    for li in range(depth):
        j = li // 2
        kind = "gla" if li % 2 == 0 else "hgrn"
        heads = gla_heads if kind == "gla" else hgrn_heads
        h, qkg, v, og = _pre(h, kind, w["ffn1"], (w["mix_norm"],) + w[kind + "_in"], li, j, tm)
        scale = (qkg.shape[1] // (3 * heads)) ** -0.5 if kind == "gla" else 1.0
        s0 = st_gla if kind == "gla" else st_hgrn
        o, s = _rec(qkg, v, None if s0 is None else s0[j], bsz, heads, scale)
        (new_gla if kind == "gla" else new_hgrn).append(s)
        h = _post(o, og, h, p, w[kind + "_out"], w["ffn2"], w["ple"], li, j, heads, li == depth - 1, tm)
    return h.reshape(bsz, t, d), jnp.stack(new_gla), jnp.stack(new_hgrn)


def kernel(x_prompt, x_sample, state_gla, state_hgrn, p_prompt, p_sample, ffn1_norm, ffn1_w_gu, ffn1_w_down, mix_norm, gla_w_in, gla_w_gk_up, gla_b_gk, gla_gn, gla_w_out, hgrn_w_in, hgrn_gn, hgrn_w_out, hgrn_lower_bounds, ffn2_norm, ffn2_w_gu, ffn2_w_down, ple_norm, ple_w_gate, ple_w_proj, final_norm):
    w = _prep_weights(ffn1_norm, ffn1_w_gu, ffn1_w_down, mix_norm, gla_w_in, gla_w_gk_up, gla_b_gk, gla_gn,
                      gla_w_out, hgrn_w_in, hgrn_gn, hgrn_w_out, hgrn_lower_bounds, ffn2_norm, ffn2_w_gu,
                      ffn2_w_down, ple_norm, ple_w_gate, ple_w_proj, final_norm)
    gla_heads = state_gla.shape[2]
    hgrn_heads = state_hgrn.shape[2]
    y_p, gla_p, hgrn_p = _trunk(x_prompt, p_prompt, None, None, w, gla_heads, hgrn_heads)
    y_s, gla_s, hgrn_s = _trunk(x_sample, p_sample, state_gla, state_hgrn, w, gla_heads, hgrn_heads)
    return (y_p, y_s, gla_p, gla_s, hgrn_p, hgrn_s)
```

```python
import functools
import types

import jax
import jax.numpy as jnp
from jax import lax
from jax.experimental import pallas as pl
from jax.experimental.pallas import tpu as pltpu

F32 = jnp.float32
BF16 = jnp.bfloat16
RMS_EPS = 1e-6
GLA_GATE_NORM = 16.0
LANES = 128
BF16_ROWS = 16
FF_CHUNK = 256
ROW_TILE = 512
MIN_CAST_STEPS = 16
REC_CHUNK = 64
REC_SUB = 16
REC_CHUNKS_PER_STEP = 2
REC_SEQS = 8
REC_SEQS_SINGLE = 16
SUBLANES = 8
VMEM_LIMIT = 62 * 1024 * 1024

_NT = (((1,), (1,)), ((), ()))
_TN = (((0,), (0,)), ((), ()))


def _cparams(n_axes):
    return pltpu.CompilerParams(dimension_semantics=("arbitrary",) * n_axes,
                                vmem_limit_bytes=VMEM_LIMIT)


def _resident(shape):
    nd = len(shape)
    return pl.BlockSpec(shape, lambda *_: (0,) * nd, pipeline_mode=pl.Buffered(1))


def _layer(arr, li):
    nd = arr.ndim - 1
    return pl.BlockSpec((None,) + arr.shape[1:], lambda *_: (li,) + (0,) * nd,
                        pipeline_mode=pl.Buffered(1))


def _wspec(w):
    arr, li = w
    return _resident(arr.shape) if li is None else _layer(arr, li)


def _rows(tm, width):
    return pl.BlockSpec((tm, width), lambda i: (i, 0))


def _cast_plan(jobs, n_steps):
    args, in_specs, out_specs, out_shapes = [], [], [], []
    for arr, li in jobs:
        n_layers, r, c = arr.shape
        rb = r // n_steps
        args.append(arr.reshape(n_layers, n_steps, rb, c))
        in_specs.append(pl.BlockSpec((None, None, rb, c), lambda i, li=li: (li, i, 0, 0)))
        out_specs.append(pl.BlockSpec((None, rb, c), lambda i: (i, 0, 0)))
        out_shapes.append(jax.ShapeDtypeStruct((n_steps, rb, c), BF16))
    return args, in_specs, out_specs, out_shapes


def _cast_rows(refs_in, refs_out):
    for src, dst in zip(refs_in, refs_out):
        dst[...] = src[...].astype(BF16)


def _dot(a, b, dims=None):
    if dims is None:
        return jnp.dot(a, b, preferred_element_type=F32)
    return lax.dot_general(a, b, dims, preferred_element_type=F32)


def _rms(x, w):
    ms = jnp.mean(x * x, axis=-1, keepdims=True)
    return x * lax.rsqrt(ms + RMS_EPS) * w


def _sigmoid(x):
    return 1.0 / (1.0 + jnp.exp(-x))


def _silu(x):
    return x * _sigmoid(x)


def _ffn_acc(x, nw_ref, wgu_ref, wd_ref, xn_ref, act_ref):
    d_ff = wd_ref.shape[0]
    xn_ref[...] = _rms(x, nw_ref[...]).astype(BF16)
    for c in range(d_ff // FF_CHUNK):
        lo, hi = c * FF_CHUNK, (c + 1) * FF_CHUNK
        xn = xn_ref[...]
        g = _dot(xn, wgu_ref[:, lo:hi])
        u = _dot(xn, wgu_ref[:, d_ff + lo:d_ff + hi])
        act_ref[:, lo:hi] = (_silu(g) * u).astype(BF16)
    return _dot(act_ref[...], wd_ref[...])


def _gla_proj(u, w_ref, wr_ref, wup_ref, bgk_ref, kd, vd):
    q = _dot(u, w_ref[:, 0:kd])
    k = _dot(u, w_ref[:, kd:2 * kd])
    v = _dot(u, w_ref[:, 2 * kd:2 * kd + vd])
    og = _dot(u, w_ref[:, 2 * kd + vd:2 * kd + 2 * vd])
    r = _dot(u, wr_ref[...]).astype(BF16)
    x = _dot(r, wup_ref[...]) + bgk_ref[...]
    g = (jnp.minimum(x, 0.0) - jnp.log1p(jnp.exp(-jnp.abs(x)))) * (1.0 / GLA_GATE_NORM)
    return q, k, v, g, og


def _hgrn_proj(u, w_ref, lbw_ref, fd, layer):
    lbw = lbw_ref[...]
    rows = [lbw[i:i + 1, :] for i in range(lbw.shape[0])]
    m = functools.reduce(jnp.maximum, rows)
    es = [jnp.exp(r - m) for r in rows]
    tot = functools.reduce(jnp.add, es)
    sm = [e / tot for e in es]
    lb = functools.reduce(jnp.add, sm[:layer + 1]) - sm[0]

    q = _silu(_dot(u, w_ref[:, 0:fd]))
    f = _dot(u, w_ref[:, fd:2 * fd])
    e = jnp.exp(-jnp.abs(f))
    inv = 1.0 / (1.0 + e)
    pos = f >= 0.0
    sig_f = jnp.where(pos, inv, e * inv)
    sig_nf = jnp.where(pos, e * inv, inv)
    g = jnp.log(lb + (1.0 - lb) * sig_f)
    k = (1.0 - lb) * sig_nf
    v = _dot(u, w_ref[:, 2 * fd:3 * fd])
    og = _dot(u, w_ref[:, 3 * fd:4 * fd])
    return q, k, v, g, og


def _pre_kernel(*refs, kind, hk, hv, layer, n_cast):
    x_ref, fnw_ref, wgu_ref, wd_ref, nw_ref, w_ref = refs[:6]
    if kind == "gla":
        wr_ref, wup_ref, bgk_ref = refs[6:9]
        rest = refs[9:]
    else:
        lbw_ref = refs[6]
        rest = refs[7:]
    cast_in, rest = rest[:n_cast], rest[n_cast:]
    h_ref, qkg_ref, v_ref, og_ref = rest[:4]
    cast_out, (xn_ref, acc_ref) = rest[4:4 + n_cast], rest[4 + n_cast:]
    _cast_rows(cast_in, cast_out)
    x = x_ref[...]
    h = x + 0.5 * _ffn_acc(x, fnw_ref, wgu_ref, wd_ref, xn_ref, acc_ref)
    h_ref[...] = h
    u = _rms(h, nw_ref[...]).astype(BF16)
    if kind == "gla":
        q, k, v, g, og = _gla_proj(u, w_ref, wr_ref, wup_ref, bgk_ref, hk, hv)
    else:
        q, k, v, g, og = _hgrn_proj(u, w_ref, lbw_ref, hk, layer)
    qkg_ref[:, 0:hk] = q
    qkg_ref[:, hk:2 * hk] = k
    qkg_ref[:, 2 * hk:3 * hk] = g
    v_ref[...] = v.astype(BF16)
    og_ref[...] = og


def _pre(x, kind, weights, layer, tm, cast_jobs):
    n, d = x.shape
    w_in, extra = weights[4][0], weights[5:]
    if kind == "gla":
        hk = extra[1][0].shape[-1]
        hv = (w_in.shape[-1] - 2 * hk) // 2
    else:
        hk = hv = w_in.shape[-1] // 4
    d_ff = weights[2][0].shape[-2]
    c_args, c_in, c_out, c_shapes = _cast_plan(cast_jobs, n // tm)
    outs = [((n, d), F32), ((n, 3 * hk), F32), ((n, hv), BF16), ((n, hv), F32)]
    res = pl.pallas_call(
        functools.partial(_pre_kernel, kind=kind, hk=hk, hv=hv, layer=layer, n_cast=len(cast_jobs)),
        grid=(n // tm,),
        in_specs=[_rows(tm, d)] + [_wspec(w) for w in weights] + c_in,
        out_specs=[_rows(tm, s[0][1]) for s in outs] + c_out,
        out_shape=[jax.ShapeDtypeStruct(*s) for s in outs] + c_shapes,
        scratch_shapes=[pltpu.VMEM((tm, d), BF16), pltpu.VMEM((tm, d_ff), BF16)],
        compiler_params=_cparams(1),
        name="pre_" + kind,
    )(x, *(w[0] for w in weights), *c_args)
    return res[:4], [c.reshape(-1, c.shape[-1]) for c in res[4:]]


def _split3(x):
    x1 = x.astype(BF16)
    r1 = x - x1.astype(F32)
    x2 = r1.astype(BF16)
    x3 = (r1 - x2.astype(F32)).astype(BF16)
    return x1, x2, x3


def _rec_kernel(*refs, heads, kdim, vdim, chunk, sub, scale, single, t_real):
    refs = list(refs)
    qkg_ref, v_ref = refs[:2]
    if single:
        s0_ref, o_ref, s_ref, vb_s = refs[2:6]
        scratch = refs[6:]
    else:
        o_ref, s_ref, st_ref = refs[2:5]
        scratch = refs[5:]
    c, sb = chunk, sub
    n_sub = c // sb
    if n_sub > 1:
        qx_s, scratch = scratch[0], scratch[1:]
    qhat_s, kdec_s, kvar_s, dec_s, attn_s = scratch
    if n_sub == 1:
        qx_s = qhat_s
    n = pl.program_id(1)
    hk = heads * kdim
    cur = types.SimpleNamespace(rows=slice(0, c))

    if single:
        n_seq = qkg_ref.shape[0] // t_real
        per_tile = SUBLANES // t_real
        pad_rows = jnp.zeros((c - SUBLANES, 1), F32)
        v_rows = v_ref[...].astype(F32)

        def seq_rows(ref, s):
            tile, slot = divmod(s, per_tile)
            x = ref[tile * SUBLANES:(tile + 1) * SUBLANES, :].astype(F32)
            if slot:
                x = pltpu.roll(x, SUBLANES - slot * t_real, 0)
            keep = lax.broadcasted_iota(jnp.int32, x.shape, 0) < t_real
            return jnp.concatenate([jnp.where(keep, x, 0.0), jnp.broadcast_to(pad_rows, (c - SUBLANES, x.shape[1]))],
                                   axis=0)
    else:
        n_seq = qkg_ref.shape[0]
        seq_rows = lambda ref, s: ref[s, cur.rows]

        @pl.when(n == 0)
        def _init():
            st_ref[...] = jnp.zeros_like(st_ref)

    causal = (lax.broadcasted_iota(jnp.int32, (c, c), 0) >= lax.broadcasted_iota(jnp.int32, (c, c), 1))
    tri = causal.astype(F32).astype(BF16)
    tri3 = jnp.concatenate([tri, tri, tri], axis=1)

    def cumsum(s):
        return _dot(tri3, jnp.concatenate(_split3(seq_rows(qkg_ref, s)[:, 2 * hk:]), axis=0))

    def prepare(s, b):
        qk = seq_rows(qkg_ref, s)
        q = qk[:, 0:hk] * scale
        k = qk[:, hk:2 * hk]
        if single:
            vb_s[s] = seq_rows(v_rows, s).astype(BF16)
        blk_rows = [slice(i * sb, (i + 1) * sb) for i in range(n_sub)]
        eb = [jnp.exp(b[r] if i == 0 else b[r] - b[i * sb - 1:i * sb, :]) for i, r in enumerate(blk_rows)]
        blk_dec = [e[sb - 1:sb, :] for e in eb]

        def span(lo, hi):
            out = None
            for i in range(lo, hi):
                out = blk_dec[i] if out is None else out * blk_dec[i]
            return out

        mul = lambda x, d: x if d is None else x * d
        qx = [q[r] * eb[i] for i, r in enumerate(blk_rows)]
        kdiag = [k[r] * (1.0 / eb[i]) for i, r in enumerate(blk_rows)]
        kend = [kdiag[i] * blk_dec[i] for i in range(n_sub)]
        cat = lambda parts: parts[0] if len(parts) == 1 else jnp.concatenate(parts, axis=0)
        qhat_s[s] = cat([mul(qx[i], span(0, i)) for i in range(n_sub)]).astype(BF16)
        kdec_s[s] = cat([mul(kend[i], span(i + 1, n_sub)) for i in range(n_sub)]).astype(BF16)
        if n_sub > 1:
            qx_s[s] = cat(qx).astype(BF16)
        for i in range(n_sub):
            parts = [mul(kend[j], span(j + 1, i)) for j in range(i)] + [kdiag[i]]
            kv = cat(parts).astype(BF16)
            if i + 1 < n_sub:
                kv = jnp.concatenate([kv, jnp.zeros((c - (i + 1) * sb, hk), BF16)], axis=0)
            kvar_s[s, i] = kv
        if single:
            p1, p2, p3 = (p.astype(F32) for p in _split3(b[c - 1:c, :]))
            r = lax.broadcasted_iota(jnp.int32, (BF16_ROWS, hk), 0)
            dec_s[s] = jnp.where(r == 0, p1, jnp.where(r == 1, p2, jnp.where(r == 2, p3, 0.0))).astype(BF16)
        else:
            dec_s[s] = span(0, n_sub)

    def scores(s, h):
        ks = slice(h * kdim, (h + 1) * kdim)
        parts = [_dot(qx_s[s, blk * sb:(blk + 1) * sb, ks], kvar_s[s, blk, :, ks], _NT)
                 for blk in range(n_sub)]
        attn = parts[0] if n_sub == 1 else jnp.concatenate(parts, axis=0)
        attn_s[s, h] = jnp.where(causal, attn, 0.0).astype(BF16)

    def update(s, h):
        ks = slice(h * kdim, (h + 1) * kdim)
        vs = slice(h * vdim, (h + 1) * vdim)
        vh = vb_s[s, :, vs] if single else v_ref[s, cur.rows, vs]
        o_intra = _dot(attn_s[s, h], vh)
        if single:
            st = s0_ref[s, h]
            ones = jnp.ones((BF16_ROWS, LANES), BF16)
            dec = jnp.exp(_dot(dec_s[s, :, ks], ones, _TN))
            dec = jnp.concatenate([dec] * (vdim // LANES), axis=1)
            s_ref[s, h] = st * dec + _dot(kdec_s[s, :, ks], vh, _TN)
            return o_intra + _dot(qhat_s[s, :, ks], st.astype(BF16))
        else:
            st = st_ref[s, h]
            o_ref[s, cur.rows, vs] = o_intra + _dot(qhat_s[s, :, ks], st.astype(BF16), _NT)
            st_ref[s, h] = st * dec_s[s, :, ks] + _dot(vh, kdec_s[s, :, ks], _TN)

    pairs = [(s, h) for h in range(heads) for s in range(n_seq)]

    def stages():
        bs = [cumsum(s) for s in range(n_seq)]
        for s in range(n_seq):
            prepare(s, bs[s])
        for s, h in pairs:
            scores(s, h)

    if single:
        stages()
        row = lax.broadcasted_iota(jnp.int32, (SUBLANES, vdim), 0)
        for h in range(heads):
            for tile in range(n_seq // per_tile):
                merged = None
                for slot in range(per_tile):
                    o = jnp.where(row < t_real, update(tile * per_tile + slot, h)[:SUBLANES], 0.0)
                    if slot:
                        o = pltpu.roll(o, slot * t_real, 0)
                    merged = o if merged is None else merged + o
                o_ref[tile * SUBLANES:(tile + 1) * SUBLANES, h * vdim:(h + 1) * vdim] = merged
    else:
        for lo in range(0, qkg_ref.shape[1], c):
            cur.rows = slice(lo, lo + c)
            stages()
            for s, h in pairs:
                update(s, h)

        @pl.when(n == pl.num_programs(1) - 1)
        def _final():
            for s, h in pairs:
                s_ref[s, h] = st_ref[s, h].T


def _rec(qkg, v, s0, n_seq, heads, scale):
    hk, hv = qkg.shape[1] // 3, v.shape[1]
    t = v.shape[0] // n_seq
    kdim, vdim = hk // heads, hv // heads
    single = s0 is not None
    bb = min(REC_SEQS_SINGLE if single else REC_SEQS, n_seq)
    chunk = REC_SUB if single else REC_CHUNK
    sub = REC_SUB
    n_sub = chunk // sub
    state = pl.BlockSpec((bb, heads, kdim, vdim), lambda i, j: (i, 0, 0, 0))
    if single:
        assert SUBLANES % t == 0 and bb % (SUBLANES // t) == 0
        grid = (n_seq // bb, 1)
        tok = lambda w: pl.BlockSpec((bb * t, w), lambda i, j: (i, 0))
        args = [qkg, v, s0]
        in_specs = [tok(3 * hk), tok(hv), state]
        o_shape = (n_seq * t, hv)
    else:
        step_rows = REC_CHUNKS_PER_STEP * chunk
        assert t % step_rows == 0
        grid = (n_seq // bb, t // step_rows)
        tok = lambda w: pl.BlockSpec((bb, step_rows, w), lambda i, j: (i, j, 0))
        args = [a.reshape(n_seq, t, a.shape[1]) for a in (qkg, v)]
        in_specs = [tok(3 * hk), tok(hv)]
        o_shape = (n_seq, t, hv)
    scratch = []
    if single:
        scratch.append(pltpu.VMEM((bb, chunk, hv), BF16))
    else:
        scratch.append(pltpu.VMEM((bb, heads, vdim, kdim), F32))
    if n_sub > 1:
        scratch.append(pltpu.VMEM((bb, chunk, hk), BF16))
    scratch += [
        pltpu.VMEM((bb, chunk, hk), BF16),
        pltpu.VMEM((bb, chunk, hk), BF16),
        pltpu.VMEM((bb, n_sub, chunk, hk), BF16),
        pltpu.VMEM((bb, BF16_ROWS, hk), BF16) if single else pltpu.VMEM((bb, 1, hk), F32),
        pltpu.VMEM((bb, heads, chunk, chunk), BF16),
    ]
    o, s = pl.pallas_call(
        functools.partial(_rec_kernel, heads=heads, kdim=kdim, vdim=vdim, chunk=chunk, sub=sub,
                          scale=scale, single=single, t_real=t),
        grid=grid,
        in_specs=in_specs,
        out_specs=[tok(hv), state],
        out_shape=[jax.ShapeDtypeStruct(o_shape, F32),
                   jax.ShapeDtypeStruct((n_seq, heads, kdim, vdim), F32)],
        scratch_shapes=scratch,
        compiler_params=_cparams(2),
        name="rec",
    )(*args)
    return o.reshape(n_seq * t, hv), s


def _post_kernel(o_ref, og_ref, h_ref, p_ref, gn_ref, wo_ref, nw_ref, wgu_ref, wd_ref,
                 pn_ref, pwg_ref, pwp_ref, fn_ref, *rest, heads, final, n_cast):
    cast_in, out_ref = rest[:n_cast], rest[n_cast]
    cast_out, (xn_ref, acc_ref) = rest[n_cast + 1:2 * n_cast + 1], rest[2 * n_cast + 1:]
    _cast_rows(cast_in, cast_out)
    o = o_ref[...]
    vdim = o.shape[1] // heads
    gn = gn_ref[...]
    y = jnp.concatenate([_rms(o[:, h * vdim:(h + 1) * vdim], gn) for h in range(heads)], axis=1)
    y = (y * _silu(og_ref[...])).astype(BF16)
    out_ref[...] = h_ref[...] + _dot(y, wo_ref[...])
    h2 = out_ref[...] + 0.5 * _ffn_acc(out_ref[...], nw_ref, wgu_ref, wd_ref, xn_ref, acc_ref)
    gate = _sigmoid(_dot(_rms(h2, pn_ref[...]).astype(BF16), pwg_ref[...]))
    h3 = h2 + gate * _dot(p_ref[...].astype(BF16), pwp_ref[...])
    if final:
        h3 = _rms(h3, fn_ref[...])
    out_ref[...] = h3


def _post(o, og, h, p, weights, li, heads, final, tm, cast_jobs):
    n, hv = o.shape
    d = h.shape[1]
    d_ff = weights[4][0].shape[-2]
    c_args, c_in, c_out, c_shapes = _cast_plan(cast_jobs, n // tm)
    res = pl.pallas_call(
        functools.partial(_post_kernel, heads=heads, final=final, n_cast=len(cast_jobs)),
        grid=(n // tm,),
        in_specs=[_rows(tm, hv), _rows(tm, hv), _rows(tm, d),
                  pl.BlockSpec((None, tm, p.shape[2]), lambda i: (li, i, 0))]
        + [_wspec(w) for w in weights] + c_in,
        out_specs=[_rows(tm, d)] + c_out,
        out_shape=[jax.ShapeDtypeStruct((n, d), F32)] + c_shapes,
        scratch_shapes=[pltpu.VMEM((tm, d), BF16), pltpu.VMEM((tm, d_ff), BF16)],
        compiler_params=_cparams(1),
        name="post",
    )(o, og, h, p, *(w[0] for w in weights), *c_args)
    return res[0], [c.reshape(-1, c.shape[-1]) for c in res[1:]]


class _Weights:
    def __init__(self, ffn1_norm, ffn1_w_gu, ffn1_w_down, mix_norm, gla_w_in, gla_w_gk_up, gla_b_gk, gla_gn,
                 gla_w_out, hgrn_w_in, hgrn_gn, hgrn_w_out, hgrn_lower_bounds, ffn2_norm, ffn2_w_gu,
                 ffn2_w_down, ple_norm, ple_w_gate, ple_w_proj, final_norm):
        row = lambda a: a.reshape(a.shape[0], 1, a.shape[1])
        self.vec = dict(ffn1_norm=row(ffn1_norm), ffn2_norm=row(ffn2_norm), mix_norm=row(mix_norm),
                        ple_norm=row(ple_norm), gla_gn=row(gla_gn), hgrn_gn=row(hgrn_gn),
                        final_norm=final_norm.reshape(1, -1))
        self.big = dict(ffn1_gu=ffn1_w_gu, ffn1_down=ffn1_w_down, ffn2_gu=ffn2_w_gu, ffn2_down=ffn2_w_down,
                        gla_out=gla_w_out, hgrn_in=hgrn_w_in, hgrn_out=hgrn_w_out,
                        ple_gate=ple_w_gate, ple_proj=ple_w_proj)
        self.bf16 = {}
        rank = gla_w_gk_up.shape[1]
        main = gla_w_in.shape[2] - rank
        gla_w = gla_w_in.astype(BF16)
        self.gla_in_arrays = (gla_w[:, :, :main],
                              jnp.pad(gla_w[:, :, main:], ((0, 0), (0, 0), (0, LANES - rank))),
                              jnp.pad(gla_w_gk_up, ((0, 0), (0, LANES - rank), (0, 0))).astype(BF16),
                              row(gla_b_gk))
        self.hgrn_lb = (hgrn_lower_bounds, None)

    def gla_in(self, j):
        return [(a, j) for a in self.gla_in_arrays]

    def get(self, name, layer):
        if (name, layer) not in self.bf16:
            self.bf16[name, layer] = (self.big[name][layer].astype(BF16), None)
        return self.bf16[name, layer]

    def jobs(self, wanted, n_steps):
        todo = [(n, l) for n, l in wanted
                if (n, l) not in self.bf16 and n_steps >= MIN_CAST_STEPS and self.big[n].shape[1] % n_steps == 0]
        return todo, [(self.big[n], l) for n, l in todo]

    def put(self, todo, copies):
        for key, c in zip(todo, copies):
            self.bf16[key] = (c, None)

    @staticmethod
    def pre_big(li):
        return [("ffn1_gu", li), ("ffn1_down", li)] + ([("hgrn_in", li // 2)] if li % 2 else [])

    @staticmethod
    def post_big(li):
        return [(("hgrn" if li % 2 else "gla") + "_out", li // 2), ("ffn2_gu", li), ("ffn2_down", li),
                ("ple_gate", li), ("ple_proj", li)]


def _trunk(x, p, st_gla, st_hgrn, w, gla_heads, hgrn_heads):
    bsz, t, d = x.shape
    n = bsz * t
    depth = p.shape[0]
    tm = min(ROW_TILE, n)
    n_tiles = n // tm
    p = p.reshape(depth, n, p.shape[-1])
    h = x.reshape(n, d)
    new_gla, new_hgrn = [], []
    for li in range(depth):
        j = li // 2
        kind = "gla" if li % 2 == 0 else "hgrn"
        heads = gla_heads if kind == "gla" else hgrn_heads
        pre_w = [(w.vec["ffn1_norm"], li), w.get("ffn1_gu", li), w.get("ffn1_down", li), (w.vec["mix_norm"], li)]
        pre_w += w.gla_in(j) if kind == "gla" else [w.get("hgrn_in", j), w.hgrn_lb]
        todo, jobs = w.jobs(w.post_big(li), n_tiles)
        (h, qkg, v, og), copies = _pre(h, kind, pre_w, li, tm, jobs)
        w.put(todo, copies)
        scale = (qkg.shape[1] // (3 * heads)) ** -0.5 if kind == "gla" else 1.0
        s0 = st_gla if kind == "gla" else st_hgrn
        o, s = _rec(qkg, v, None if s0 is None else s0[j], bsz, heads, scale)
        (new_gla if kind == "gla" else new_hgrn).append(s)
        post_w = [(w.vec[kind + "_gn"], j), w.get(kind + "_out", j), (w.vec["ffn2_norm"], li),
                  w.get("ffn2_gu", li), w.get("ffn2_down", li), (w.vec["ple_norm"], li),
                  w.get("ple_gate", li), w.get("ple_proj", li), (w.vec["final_norm"], None)]
        todo, jobs = w.jobs(w.pre_big(li + 1), n_tiles) if li + 1 < depth else ([], [])
        h, copies = _post(o, og, h, p, post_w, li, heads, li == depth - 1, tm, jobs)
        w.put(todo, copies)
    return h.reshape(bsz, t, d), jnp.stack(new_gla), jnp.stack(new_hgrn)


def kernel(x_prompt, x_sample, state_gla, state_hgrn, p_prompt, p_sample, ffn1_norm, ffn1_w_gu, ffn1_w_down, mix_norm, gla_w_in, gla_w_gk_up, gla_b_gk, gla_gn, gla_w_out, hgrn_w_in, hgrn_gn, hgrn_w_out, hgrn_lower_bounds, ffn2_norm, ffn2_w_gu, ffn2_w_down, ple_norm, ple_w_gate, ple_w_proj, final_norm):
    w = _Weights(ffn1_norm, ffn1_w_gu, ffn1_w_down, mix_norm, gla_w_in, gla_w_gk_up, gla_b_gk, gla_gn,
                 gla_w_out, hgrn_w_in, hgrn_gn, hgrn_w_out, hgrn_lower_bounds, ffn2_norm, ffn2_w_gu,
                 ffn2_w_down, ple_norm, ple_w_gate, ple_w_proj, final_norm)
    gla_heads = state_gla.shape[2]
    hgrn_heads = state_hgrn.shape[2]
    y_p, gla_p, hgrn_p = _trunk(x_prompt, p_prompt, None, None, w, gla_heads, hgrn_heads)
    y_s, gla_s, hgrn_s = _trunk(x_sample, p_sample, state_gla, state_hgrn, w, gla_heads, hgrn_heads)
    return (y_p, y_s, gla_p, gla_s, hgrn_p, hgrn_s)
```
